```python
import jax, jax.numpy as jnp
from jax import lax
import numpy as np

D_MODEL = 1024
BATCH = 16
SEQ = 4096
DEPTH = 1

SSM_HEADS = 16
SSM_HEAD_DIM = 64
SSM_WIDTH = SSM_HEADS * SSM_HEAD_DIM
SSM_GROUPS = 2
SSM_STATE = 128
SSM_CONV = 4
SSM_CHUNK = 256
SSM_CONV_WIDTH = SSM_WIDTH + 2 * SSM_GROUPS * SSM_STATE
MLA_HEADS = 8
MLA_Q_RANK = 384
MLA_KV_RANK = 256
MLA_NOPE_DIM = 128
MLA_ROPE_DIM = 64
MLA_V_DIM = 128
MLA_WIDTH = MLA_HEADS * MLA_V_DIM
MIX_WIDTH = SSM_WIDTH + MLA_WIDTH
ATTN_BLOCK = 128
ROPE_THETA = 10000.0
IN_SPLITS = (SSM_WIDTH, SSM_CONV_WIDTH, SSM_HEADS, MLA_Q_RANK, MLA_KV_RANK, MLA_ROPE_DIM)
IN_WIDTH = sum(IN_SPLITS)
PEER_HEADS = 8
PEER_N_KEYS = 128
PEER_N_EXPERTS = PEER_N_KEYS ** 2
PEER_HALF_DIM = 128
PEER_TOPK = 16
PEER_TOKEN_BLOCK = 128
DEEPNORM_ALPHA = (2.0 * DEPTH) ** 0.25
DEEPNORM_BETA = (8.0 * DEPTH) ** -0.25
NORM_EPS = 1e-5

kernel_name = 'hybrid_ssd_mla_peer_block'


def layer_norm(x, g, b):
    xf = x.astype(jnp.float32)
    mu = jnp.mean(xf, -1, keepdims=True)
    var = jnp.mean(jnp.square(xf - mu), -1, keepdims=True)
    return ((xf - mu) * lax.rsqrt(var + NORM_EPS) * g + b).astype(x.dtype)


def rms_norm(x, g):
    xf = x.astype(jnp.float32)
    return (xf * lax.rsqrt(jnp.mean(xf * xf, -1, keepdims=True) + NORM_EPS) * g).astype(x.dtype)


def apply_rope(x, cos, sin):
    x1, x2 = jnp.split(x.astype(jnp.float32), 2, axis=-1)
    return jnp.concatenate([x1 * cos - x2 * sin, x2 * cos + x1 * sin], axis=-1).astype(x.dtype)


def causal_dwconv(u, w, b):
    out = lax.conv_general_dilated(u, w[:, None, :], window_strides=(1,),
                                   padding=[(SSM_CONV - 1, 0)],
                                   dimension_numbers=('NWC', 'WIO', 'NWC'),
                                   feature_group_count=u.shape[-1])
    return out + b


def ssd_chunked(xs, dt, A, Bm, Cm):
    b, L, H, P = xs.shape
    G, N = Bm.shape[2], Bm.shape[3]
    E = H // G
    Q = SSM_CHUNK
    pad = (-L) % Q
    padseq = lambda t: jnp.pad(t, [(0, 0), (0, pad)] + [(0, 0)] * (t.ndim - 2))
    xs, dt, Bm, Cm = padseq(xs), padseq(dt), padseq(Bm), padseq(Cm)
    nc = (L + pad) // Q
    xdt = (xs * dt[..., None]).reshape(b, nc, Q, G, E, P)
    a = jnp.moveaxis((dt * A).reshape(b, nc, Q, G, E), 2, -1)
    a_cum = jnp.cumsum(a, axis=-1)
    Bc = Bm.reshape(b, nc, Q, G, N)
    Cc = Cm.reshape(b, nc, Q, G, N)
    causal = jnp.tril(jnp.ones((Q, Q), dtype=bool))
    seg = a_cum[..., :, None] - a_cum[..., None, :]
    decay = jnp.exp(jnp.where(causal, seg, -jnp.inf))
    cb = jnp.einsum('bclgn,bcsgn->bcgls', Cc, Bc)
    y_diag = jnp.einsum('bcgels,bcsgep->bclgep', cb[:, :, :, None] * decay, xdt)
    decay_states = jnp.exp(a_cum[..., -1:] - a_cum)
    states = jnp.einsum('bclgn,bcgel,bclgep->bcgepn', Bc, decay_states, xdt)
    chunk_decay = jnp.exp(a_cum[..., -1])

    def step(h, inp):
        s_c, d_c = inp
        return h * d_c[..., None, None] + s_c, h

    init = jnp.zeros((b, G, E, P, N), xdt.dtype)
    _, prev = lax.scan(step, init, (jnp.moveaxis(states, 1, 0), jnp.moveaxis(chunk_decay, 1, 0)))
    prev = jnp.moveaxis(prev, 0, 1)
    y_off = jnp.einsum('bclgn,bcgepn,bcgel->bclgep', Cc, prev, jnp.exp(a_cum))
    return (y_diag + y_off).reshape(b, nc * Q, H, P)[:, :L]


def mla_heads(q_lat, kv_lat, k_pe, cos, sin, q_norm_g, w_q_b, kv_norm_g, w_kv_b):
    B, S, _ = q_lat.shape
    q = (rms_norm(q_lat, q_norm_g) @ w_q_b).reshape(B, S, MLA_HEADS, MLA_NOPE_DIM + MLA_ROPE_DIM)
    q_nope = q[..., :MLA_NOPE_DIM]
    q_pe = apply_rope(q[..., MLA_NOPE_DIM:], cos[:, None], sin[:, None])
    kv = (rms_norm(kv_lat, kv_norm_g) @ w_kv_b).reshape(B, S, MLA_HEADS, MLA_NOPE_DIM + MLA_V_DIM)
    k_nope, v = kv[..., :MLA_NOPE_DIM], kv[..., MLA_NOPE_DIM:]
    k_pe = apply_rope(k_pe, cos, sin)
    scale = (MLA_NOPE_DIM + MLA_ROPE_DIM) ** -0.5
    nb = S // ATTN_BLOCK
    qn_blocks = jnp.moveaxis(q_nope.reshape(B, nb, ATTN_BLOCK, MLA_HEADS, MLA_NOPE_DIM), 1, 0)
    qp_blocks = jnp.moveaxis(q_pe.reshape(B, nb, ATTN_BLOCK, MLA_HEADS, MLA_ROPE_DIM), 1, 0)
    key_pos = jnp.arange(S)

    def attend(args):
        qn, qp, blk = args
        s = (jnp.einsum('bqhd,bkhd->bhqk', qn, k_nope).astype(jnp.float32)
             + jnp.einsum('bqhr,bkr->bhqk', qp, k_pe).astype(jnp.float32)) * scale
        q_pos = blk * ATTN_BLOCK + jnp.arange(ATTN_BLOCK)
        s = jnp.where(key_pos[None, :] <= q_pos[:, None], s, -jnp.inf)
        p = jax.nn.softmax(s, axis=-1).astype(v.dtype)
        return jnp.einsum('bhqk,bkhd->bqhd', p, v)

    o = lax.map(attend, (qn_blocks, qp_blocks, jnp.arange(nb)))
    return jnp.moveaxis(o, 0, 1).reshape(B, S, MLA_WIDTH)


def hybrid_mixer(h, cos, sin, w_in, conv_w, conv_b, dt_bias, a_log, d_skip, ssm_norm_g,
                 q_norm_g, w_q_b, kv_norm_g, w_kv_b, w_out):
    B, S, _ = h.shape
    proj = h @ w_in
    z, xbc, dt_raw, q_lat, kv_lat, k_pe = jnp.split(proj, np.cumsum(IN_SPLITS)[:-1].tolist(), axis=-1)
    xbc = jax.nn.silu(causal_dwconv(xbc, conv_w, conv_b))
    xs, Bm, Cm = jnp.split(xbc, [SSM_WIDTH, SSM_WIDTH + SSM_GROUPS * SSM_STATE], axis=-1)
    xs = xs.reshape(B, S, SSM_HEADS, SSM_HEAD_DIM).astype(jnp.float32)
    dt = jax.nn.softplus(dt_raw.astype(jnp.float32) + dt_bias.astype(jnp.float32))
    A = -jnp.exp(a_log.astype(jnp.float32))
    y = ssd_chunked(xs, dt, A,
                    Bm.reshape(B, S, SSM_GROUPS, SSM_STATE).astype(jnp.float32),
                    Cm.reshape(B, S, SSM_GROUPS, SSM_STATE).astype(jnp.float32))
    y = y + d_skip.astype(jnp.float32)[:, None] * xs
    y_ssm = rms_norm(y.reshape(B, S, SSM_WIDTH) * jax.nn.silu(z.astype(jnp.float32)), ssm_norm_g).astype(h.dtype)
    y_mla = mla_heads(q_lat, kv_lat, k_pe, cos, sin, q_norm_g, w_q_b, kv_norm_g, w_kv_b).astype(h.dtype)
    return jnp.concatenate([y_ssm, y_mla], axis=-1) @ w_out


def peer(x, w_query, sub_keys, u_table, v_table):
    B, S, D = x.shape
    T = PEER_TOKEN_BLOCK
    K = PEER_TOPK
    xt = x.reshape(-1, T, D)

    def block(xb):
        q = (xb @ w_query).reshape(T, PEER_HEADS, 2, PEER_HALF_DIM)
        s = jnp.einsum('thkd,hknd->thkn', q, sub_keys).astype(jnp.float32)
        sv, si = lax.top_k(s, K)
        cand_s = (sv[:, :, 0, :, None] + sv[:, :, 1, None, :]).reshape(T, PEER_HEADS, K * K)
        cand_i = (si[:, :, 0, :, None] * PEER_N_KEYS + si[:, :, 1, None, :]).reshape(T, PEER_HEADS, K * K)
        top_s, top_pos = lax.top_k(cand_s, K)
        expert_idx = jnp.take_along_axis(cand_i, top_pos, axis=-1)
        gate = jax.nn.softmax(top_s, axis=-1)
        u = jnp.take(u_table, expert_idx, axis=0)
        act = jax.nn.gelu(jnp.einsum('td,thkd->thk', xb, u).astype(jnp.float32), approximate=False)
        v = jnp.take(v_table, expert_idx, axis=0)
        return jnp.einsum('thk,thkd->td', (gate * act).astype(xb.dtype), v)

    return lax.map(block, xt).reshape(B, S, D)


def setup_inputs(seed: int = 0) -> dict:
    key = jax.random.key(seed)
    ks = jax.random.split(key, 24)
    f32 = jnp.float32
    L = DEPTH
    nrm = lambda k, shape, s: jax.random.normal(k, shape, f32) * s
    gain = lambda k, shape: 1.0 + 0.02 * jax.random.normal(k, shape, f32)
    dt0 = jnp.exp(jax.random.uniform(ks[6], (L, SSM_HEADS), f32, np.float32(np.log(1e-3)), np.float32(np.log(1e-1))))
    dt_bias = dt0 + jnp.log(-jnp.expm1(-dt0))
    return {
        'x': jax.random.normal(ks[0], (BATCH, SEQ, D_MODEL), f32),
        'ln_in_g': gain(ks[1], (D_MODEL,)),
        'ln_in_b': nrm(ks[2], (D_MODEL,), 0.02),
        'w_in': nrm(ks[3], (L, D_MODEL, IN_WIDTH), D_MODEL ** -0.5),
        'conv_w': nrm(ks[4], (L, SSM_CONV, SSM_CONV_WIDTH), SSM_CONV ** -0.5),
        'conv_b': nrm(ks[5], (L, SSM_CONV_WIDTH), 0.02),
        'dt_bias': dt_bias,
        'a_log': jnp.log(jax.random.uniform(ks[7], (L, SSM_HEADS), f32, 1.0, 16.0)),
        'd_skip': gain(ks[8], (L, SSM_HEADS)),
        'ssm_norm_g': gain(ks[9], (L, SSM_WIDTH)),
        'q_norm_g': gain(ks[10], (L, MLA_Q_RANK)),
        'w_q_b': nrm(ks[11], (L, MLA_Q_RANK, MLA_HEADS * (MLA_NOPE_DIM + MLA_ROPE_DIM)), MLA_Q_RANK ** -0.5),
        'kv_norm_g': gain(ks[12], (L, MLA_KV_RANK)),
        'w_kv_b': nrm(ks[13], (L, MLA_KV_RANK, MLA_HEADS * (MLA_NOPE_DIM + MLA_V_DIM)), MLA_KV_RANK ** -0.5),
        'w_out': nrm(ks[14], (L, MIX_WIDTH, D_MODEL), DEEPNORM_BETA * MIX_WIDTH ** -0.5),
        'ln_mix_g': gain(ks[15], (L, D_MODEL)),
        'ln_mix_b': nrm(ks[16], (L, D_MODEL), 0.02),
        'w_query': nrm(ks[17], (L, D_MODEL, PEER_HEADS * 2 * PEER_HALF_DIM), D_MODEL ** -0.5),
        'sub_keys': nrm(ks[18], (L, PEER_HEADS, 2, PEER_N_KEYS, PEER_HALF_DIM), PEER_HALF_DIM ** -0.5),
        'u_table': nrm(ks[19], (L, PEER_N_EXPERTS, D_MODEL), D_MODEL ** -0.5),
        'v_table': nrm(ks[20], (L, PEER_N_EXPERTS, D_MODEL), DEEPNORM_BETA * PEER_HEADS ** -0.5),
        'ln_ffn_g': gain(ks[21], (L, D_MODEL)),
        'ln_ffn_b': nrm(ks[22], (L, D_MODEL), 0.02),
    }


def reference(x, ln_in_g, ln_in_b, w_in, conv_w, conv_b, dt_bias, a_log, d_skip, ssm_norm_g,
              q_norm_g, w_q_b, kv_norm_g, w_kv_b, w_out, ln_mix_g, ln_mix_b,
              w_query, sub_keys, u_table, v_table, ln_ffn_g, ln_ffn_b):
    S = x.shape[1]
    half = MLA_ROPE_DIM // 2
    inv_freq = ROPE_THETA ** (-jnp.arange(half, dtype=jnp.float32) / half)
    ang = jnp.arange(S, dtype=jnp.float32)[:, None] * inv_freq
    cos, sin = jnp.cos(ang), jnp.sin(ang)
    h = layer_norm(x, ln_in_g, ln_in_b)
    for l in range(DEPTH):
        mix = hybrid_mixer(h, cos, sin, w_in[l], conv_w[l], conv_b[l], dt_bias[l], a_log[l], d_skip[l],
                           ssm_norm_g[l], q_norm_g[l], w_q_b[l], kv_norm_g[l], w_kv_b[l], w_out[l])
        h = layer_norm(DEEPNORM_ALPHA * h + mix, ln_mix_g[l], ln_mix_b[l])
        ffn = peer(h, w_query[l], sub_keys[l], u_table[l], v_table[l])
        h = layer_norm(DEEPNORM_ALPHA * h + ffn, ln_ffn_g[l], ln_ffn_b[l])
    return h
```

```python
import functools

import numpy as np
import jax
import jax.numpy as jnp
from jax import lax
from jax.experimental import pallas as pl
from jax.experimental.pallas import tpu as pltpu

F32 = jnp.float32
BF16 = jnp.bfloat16
HIGHEST = lax.Precision.HIGHEST

D_MODEL = 1024
SSM_HEADS = 16
SSM_HEAD_DIM = 64
SSM_WIDTH = SSM_HEADS * SSM_HEAD_DIM
SSM_GROUPS = 2
SSM_STATE = 128
SSM_CONV = 4
SSM_CHUNK = 256
SSM_CONV_WIDTH = SSM_WIDTH + 2 * SSM_GROUPS * SSM_STATE
MLA_HEADS = 8
MLA_Q_RANK = 384
MLA_KV_RANK = 256
MLA_NOPE = 128
MLA_ROPE = 64
MLA_V = 128
MLA_WIDTH = MLA_HEADS * MLA_V
ROPE_THETA = 10000.0
PEER_HEADS = 8
PEER_KEYS = 128
PEER_HALF = 128
PEER_TOPK = 16
PEER_SEL = PEER_HEADS * PEER_TOPK
DEPTH = 1
ALPHA = (2.0 * DEPTH) ** 0.25
EPS = 1e-5

LANES = 128
SUBLANES = 8
MXU_DEPTH = 256
VMEM_LIMIT = 48 * 1024 * 1024
VMEM_LIMIT_TABLE = 56 * 1024 * 1024

ROW_TILE = 512
ATTN_BLOCK = 512
ROUTE_TILE = 128
GATHER_TILE = 128
GATHER_GROUP = 4
TILE_STRIDE = PEER_SEL + 1
WORDS_PER_ROW = D_MODEL // 2 // LANES

_OFF_Z = 0
_OFF_XBC = _OFF_Z + SSM_WIDTH
_OFF_QL = _OFF_XBC + SSM_CONV_WIDTH
_OFF_KVL = _OFF_QL + MLA_Q_RANK
_OFF_KPE = _OFF_KVL + MLA_KV_RANK
_OFF_DT = _OFF_KPE + LANES
_IN_COLS = _OFF_DT + LANES


def _params(n_grid):
    return pltpu.CompilerParams(dimension_semantics=("arbitrary",) * n_grid, vmem_limit_bytes=VMEM_LIMIT)


def _layer_norm(x, g, b):
    mu = jnp.mean(x, axis=-1, keepdims=True)
    xc = x - mu
    var = jnp.mean(xc * xc, axis=-1, keepdims=True)
    return xc * lax.rsqrt(var + EPS) * g + b


def _rms_norm(x, g):
    return x * lax.rsqrt(jnp.mean(x * x, axis=-1, keepdims=True) + EPS) * g


def _sigmoid(x):
    return 1.0 / (1.0 + jnp.exp(-x))


def _softplus(x):
    return jnp.maximum(x, 0.0) + jnp.log1p(jnp.exp(-jnp.abs(x)))


def _inproj_kernel(x_ref, g_ref, b_ref, w_ref, z_ref, xbc_ref, ql_ref, kvl_ref, kpe_ref, dt_ref):
    h = _layer_norm(x_ref[...], g_ref[...], b_ref[...]).astype(BF16)

    def seg(lo, hi):
        return jnp.dot(h, w_ref[:, lo:hi], preferred_element_type=F32)

    z_ref[...] = seg(_OFF_Z, _OFF_XBC)
    xbc_ref[...] = seg(_OFF_XBC, _OFF_QL)
    ql_ref[...] = seg(_OFF_QL, _OFF_KVL)
    kvl_ref[...] = seg(_OFF_KVL, _OFF_KPE)
    kpe_ref[...] = seg(_OFF_KPE, _OFF_DT)
    dt_ref[...] = seg(_OFF_DT, _IN_COLS)


def _inproj(x2, ln_g, ln_b, w_packed):
    T = x2.shape[0]
    widths = (SSM_WIDTH, SSM_CONV_WIDTH, MLA_Q_RANK, MLA_KV_RANK, LANES, LANES)
    row = lambda w: pl.BlockSpec((ROW_TILE, w), lambda i: (i, 0))
    full = lambda a: pl.BlockSpec(a.shape, lambda i: (0,) * a.ndim)
    return pl.pallas_call(
        _inproj_kernel,
        grid=(T // ROW_TILE,),
        in_specs=[row(D_MODEL), full(ln_g), full(ln_b), full(w_packed)],
        out_specs=[row(w) for w in widths],
        out_shape=[jax.ShapeDtypeStruct((T, w), F32) for w in widths],
        compiler_params=_params(1),
        name="inproj",
    )(x2, ln_g, ln_b, w_packed)


def _ssd_kernel(z_ref, xbc_ref, dt_ref, dtT_ref, convw_ref, convb_ref, dtb_ref, dtbT_ref, alog_ref, alogT_ref,
                dskip_ref, g_ref, e_ref, y_ref, xpad_ref, st_ref):
    Q = SSM_CHUNK
    HALO = SUBLANES
    c = pl.program_id(1)

    @pl.when(c == 0)
    def _():
        xpad_ref[0:HALO, :] = jnp.zeros((HALO, SSM_CONV_WIDTH), F32)
        st_ref[...] = jnp.zeros(st_ref.shape, F32)

    raw = xbc_ref[...]
    xpad_ref[HALO:HALO + Q, :] = raw
    conv = convb_ref[...] + raw * convw_ref[SSM_CONV - 1:SSM_CONV, :]
    for k in range(SSM_CONV - 1):
        off = HALO - (SSM_CONV - 1) + k
        conv = conv + xpad_ref[off:off + Q, :] * convw_ref[k:k + 1, :]
    xpad_ref[0:HALO, :] = raw[Q - HALO:Q, :]
    xbc = conv * _sigmoid(conv)
    xs = xbc[:, :SSM_WIDTH]
    bm = xbc[:, SSM_WIDTH:SSM_WIDTH + SSM_GROUPS * SSM_STATE]
    cm = xbc[:, SSM_WIDTH + SSM_GROUPS * SSM_STATE:]

    dt = _softplus(dt_ref[...] + dtb_ref[...])
    dtT = _softplus(dtT_ref[...] + dtbT_ref[...])
    a = dt * (-jnp.exp(alog_ref[...]))
    aT = dtT * (-jnp.exp(alogT_ref[...]))
    row = lax.broadcasted_iota(jnp.int32, (Q, Q), 0)
    col = lax.broadcasted_iota(jnp.int32, (Q, Q), 1)
    causal = col <= row
    lower = causal.astype(F32)
    upper = (row <= col).astype(F32)
    a_cum = jnp.dot(lower, a, precision=HIGHEST, preferred_element_type=F32)
    a_cumT = jnp.dot(aT, upper, precision=HIGHEST, preferred_element_type=F32)
    a_last = a_cum[Q - 1:Q, :]

    expand = e_ref[...]
    ex = lambda v: jnp.dot(v, expand, precision=HIGHEST, preferred_element_type=F32)
    xdt = xs * ex(dt)
    decay_out = ex(jnp.exp(a_cum))
    decay_st = ex(jnp.exp(a_last - a_cum))
    chunk_decay = ex(jnp.broadcast_to(jnp.exp(a_last), (SUBLANES, LANES)))[0:1, :]
    xdt_b = xdt.astype(BF16)
    wx_b = (decay_st * xdt).astype(BF16)

    heads_per_group = SSM_HEADS // SSM_GROUPS
    gw = heads_per_group * SSM_HEAD_DIM
    y_parts = []
    for g in range(SSM_GROUPS):
        bg = bm[:, g * SSM_STATE:(g + 1) * SSM_STATE]
        cg = cm[:, g * SSM_STATE:(g + 1) * SSM_STATE].astype(BF16)
        cb = lax.dot_general(cg, bg.astype(BF16), (((1,), (1,)), ((), ())), preferred_element_type=F32)
        yd = []
        for e in range(heads_per_group):
            h = g * heads_per_group + e
            seg = a_cum[:, h:h + 1] - a_cumT[h:h + 1, :]
            m = (cb * jnp.exp(jnp.where(causal, seg, -jnp.inf))).astype(BF16)
            yd.append(jnp.dot(m, xdt_b[:, h * SSM_HEAD_DIM:(h + 1) * SSM_HEAD_DIM], preferred_element_type=F32))
        yd = jnp.concatenate(yd, axis=1)
        st = st_ref[g]
        y_off = jnp.dot(cg, st.astype(BF16), preferred_element_type=F32) * decay_out[:, g * gw:(g + 1) * gw]
        y_parts.append(yd + y_off)
        bgT = jnp.transpose(bg).astype(BF16)
        st_ref[g] = st * chunk_decay[:, g * gw:(g + 1) * gw] + jnp.dot(
            bgT, wx_b[:, g * gw:(g + 1) * gw], preferred_element_type=F32)
    y = jnp.concatenate(y_parts, axis=1)
    d_e = ex(jnp.broadcast_to(dskip_ref[...], (SUBLANES, LANES)))[0:1, :]
    y = y + d_e * xs
    zz = z_ref[...]
    y = y * (zz * _sigmoid(zz))
    y_ref[...] = _rms_norm(y, g_ref[...]).astype(BF16)


def _ssd(z, xbc, dt, dtT, conv_w, conv_b, dt_bias, a_log, d_skip, norm_g, batch, seq):
    T = z.shape[0]
    nc = seq // SSM_CHUNK
    pad = lambda v: jnp.pad(v.reshape(1, -1), ((0, 0), (0, LANES - v.shape[-1])))
    colv = lambda v: v.reshape(-1, 1)
    head_of_lane = np.arange(SSM_WIDTH) // SSM_HEAD_DIM
    expand = jnp.asarray((np.arange(LANES)[:, None] == head_of_lane[None, :]).astype(np.float32))
    row = lambda w: pl.BlockSpec((SSM_CHUNK, w), lambda b, c: (b * nc + c, 0))
    full = lambda a: pl.BlockSpec(a.shape, lambda b, c: (0,) * a.ndim)
    args = (z, xbc, dt, dtT, conv_w, conv_b.reshape(1, -1), pad(dt_bias), colv(dt_bias), pad(a_log), colv(a_log),
            pad(d_skip), norm_g.reshape(1, -1), expand)
    in_specs = [row(SSM_WIDTH), row(SSM_CONV_WIDTH), row(LANES),
                pl.BlockSpec((None, SSM_HEADS, SSM_CHUNK), lambda b, c: (b, 0, c))]
    in_specs += [full(a) for a in args[4:]]
    return pl.pallas_call(
        _ssd_kernel,
        grid=(batch, nc),
        in_specs=in_specs,
        out_specs=row(SSM_WIDTH),
        out_shape=jax.ShapeDtypeStruct((T, SSM_WIDTH), BF16),
        scratch_shapes=[pltpu.VMEM((SUBLANES + SSM_CHUNK, SSM_CONV_WIDTH), F32),
                        pltpu.VMEM((SSM_GROUPS, SSM_STATE, SSM_WIDTH // SSM_GROUPS), F32)],
        compiler_params=_params(2),
        name="ssd",
    )(*args)


def _mlaproj_kernel(ql_ref, kvl_ref, kpe_ref, cos_ref, sin_ref, qg_ref, kvg_ref, wqn_ref, wqp_ref, wqs_ref,
                    wkn_ref, wv_ref, q_ref, k_ref, v_ref):
    scale = (MLA_NOPE + MLA_ROPE) ** -0.5
    cos = cos_ref[...]
    sin = sin_ref[...]
    qn = _rms_norm(ql_ref[...], qg_ref[...]).astype(BF16)
    q_nope = jnp.dot(qn, wqn_ref[...], preferred_element_type=F32) * scale
    q_pe = (jnp.dot(qn, wqp_ref[...], preferred_element_type=F32) * cos
            + jnp.dot(qn, wqs_ref[...], preferred_element_type=F32) * sin) * scale
    kvn = _rms_norm(kvl_ref[...], kvg_ref[...]).astype(BF16)
    k_nope = jnp.dot(kvn, wkn_ref[...], preferred_element_type=F32)
    v = jnp.dot(kvn, wv_ref[...], preferred_element_type=F32)
    kpe = kpe_ref[...]
    k_pe = kpe[:, :MLA_ROPE] * cos[:, :MLA_ROPE] + kpe[:, MLA_ROPE:] * sin[:, :MLA_ROPE]
    zpad = jnp.zeros((q_nope.shape[0], MXU_DEPTH - MLA_NOPE - MLA_ROPE), F32)
    for h in range(MLA_HEADS):
        q_ref[h] = jnp.concatenate(
            [q_nope[:, h * MLA_NOPE:(h + 1) * MLA_NOPE], q_pe[:, h * MLA_ROPE:(h + 1) * MLA_ROPE], zpad],
            axis=1).astype(BF16)
        k_ref[h] = jnp.concatenate([k_nope[:, h * MLA_NOPE:(h + 1) * MLA_NOPE], k_pe, zpad], axis=1).astype(BF16)
        v_ref[h] = v[:, h * MLA_V:(h + 1) * MLA_V].astype(BF16)


def _mlaproj(ql, kvl, kpe, cosq, sinq, q_g, kv_g, wqn, wqp, wqs, wkn, wv, batch, seq):
    T = ql.shape[0]
    per_seq = seq // ROW_TILE
    row = lambda w: pl.BlockSpec((ROW_TILE, w), lambda i: (i, 0))
    pos = lambda w: pl.BlockSpec((ROW_TILE, w), lambda i: (i % per_seq, 0))
    full = lambda a: pl.BlockSpec(a.shape, lambda i: (0,) * a.ndim)
    head = lambda w: pl.BlockSpec((None, MLA_HEADS, ROW_TILE, w), lambda i: (i // per_seq, 0, i % per_seq, 0))
    consts = (q_g.reshape(1, -1), kv_g.reshape(1, -1), wqn, wqp, wqs, wkn, wv)
    return pl.pallas_call(
        _mlaproj_kernel,
        grid=(T // ROW_TILE,),
        in_specs=[row(MLA_Q_RANK), row(MLA_KV_RANK), row(LANES), pos(MLA_HEADS * MLA_ROPE), pos(MLA_HEADS * MLA_ROPE)]
        + [full(a) for a in consts],
        out_specs=[head(MXU_DEPTH), head(MXU_DEPTH), head(MLA_V)],
        out_shape=[jax.ShapeDtypeStruct((batch, MLA_HEADS, seq, MXU_DEPTH), BF16),
                   jax.ShapeDtypeStruct((batch, MLA_HEADS, seq, MXU_DEPTH), BF16),
                   jax.ShapeDtypeStruct((batch, MLA_HEADS, seq, MLA_V), BF16)],
        compiler_params=_params(1),
        name="mlaproj",
    )(ql, kvl, kpe, cosq, sinq, *consts)


def _flash_kernel(q_ref, k_ref, v_ref, o_ref, m_ref, l_ref, acc_ref):
    qi = pl.program_id(2)
    kj = pl.program_id(3)
    blk = ATTN_BLOCK

    @pl.when(kj == 0)
    def _():
        m_ref[...] = jnp.full(m_ref.shape, -jnp.inf, F32)
        l_ref[...] = jnp.zeros(l_ref.shape, F32)
        acc_ref[...] = jnp.zeros(acc_ref.shape, F32)

    @pl.when(kj <= qi)
    def _():
        s = lax.dot_general(q_ref[...], k_ref[...], (((1,), (1,)), ((), ())), preferred_element_type=F32)
        q_pos = qi * blk + lax.broadcasted_iota(jnp.int32, (blk, blk), 0)
        k_pos = kj * blk + lax.broadcasted_iota(jnp.int32, (blk, blk), 1)
        s = jnp.where(k_pos <= q_pos, s, -jnp.inf)
        m_prev = m_ref[...]
        m_new = jnp.maximum(m_prev, jnp.max(s, axis=-1, keepdims=True))
        p = jnp.exp(s - m_new)
        alpha = jnp.exp(m_prev - m_new)
        l_ref[...] = alpha * l_ref[...] + jnp.sum(p, axis=-1, keepdims=True)
        acc_ref[...] = alpha * acc_ref[...] + jnp.dot(p.astype(BF16), v_ref[...], preferred_element_type=F32)
        m_ref[...] = m_new

    @pl.when(kj == qi)
    def _():
        o_ref[...] = (acc_ref[...] / l_ref[...]).astype(BF16)


def _flash(q, k, v):
    batch, heads, seq, _ = q.shape
    nb = seq // ATTN_BLOCK
    qspec = pl.BlockSpec((None, None, ATTN_BLOCK, MXU_DEPTH), lambda b, h, i, j: (b, h, i, 0))
    kspec = pl.BlockSpec((None, None, ATTN_BLOCK, MXU_DEPTH), lambda b, h, i, j: (b, h, jnp.minimum(i, j), 0))
    vspec = pl.BlockSpec((None, None, ATTN_BLOCK, MLA_V), lambda b, h, i, j: (b, h, jnp.minimum(i, j), 0))
    return pl.pallas_call(
        _flash_kernel,
        grid=(batch, heads, nb, nb),
        in_specs=[qspec, kspec, vspec],
        out_specs=pl.BlockSpec((None, ATTN_BLOCK, MLA_V), lambda b, h, i, j: (b, i, h)),
        out_shape=jax.ShapeDtypeStruct((batch, seq, heads * MLA_V), BF16),
        scratch_shapes=[pltpu.VMEM((ATTN_BLOCK, 1), F32), pltpu.VMEM((ATTN_BLOCK, 1), F32),
                        pltpu.VMEM((ATTN_BLOCK, MLA_V), F32)],
        compiler_params=_params(4),
        name="flash",
    )(q, k, v)


def _outproj_kernel(x_ref, ys_ref, ym_ref, g0_ref, b0_ref, wa_ref, wb_ref, g1_ref, b1_ref, o_ref):
    h = _layer_norm(x_ref[...], g0_ref[...], b0_ref[...])
    mix = (jnp.dot(ys_ref[...], wa_ref[...], preferred_element_type=F32)
           + jnp.dot(ym_ref[...], wb_ref[...], preferred_element_type=F32))
    o_ref[...] = _layer_norm(ALPHA * h + mix, g1_ref[...], b1_ref[...])


def _outproj(x2, y_ssm, y_mla, g0, b0, wa, wb, g1, b1):
    T = x2.shape[0]
    row = lambda w: pl.BlockSpec((ROW_TILE, w), lambda i: (i, 0))
    full = lambda a: pl.BlockSpec(a.shape, lambda i: (0,) * a.ndim)
    consts = (g0, b0, wa, wb, g1, b1)
    return pl.pallas_call(
        _outproj_kernel,
        grid=(T // ROW_TILE,),
        in_specs=[row(D_MODEL), row(SSM_WIDTH), row(MLA_WIDTH)] + [full(a) for a in consts],
        out_specs=row(D_MODEL),
        out_shape=jax.ShapeDtypeStruct((T, D_MODEL), F32),
        compiler_params=_params(1),
        name="outproj",
    )(x2, y_ssm, y_mla, *consts)


def _top_rows(s, k, payload=None):
    n = s.shape[0]
    rows = lax.broadcasted_iota(jnp.int32, s.shape, 0).astype(F32)
    vals, ids = [], []
    for _ in range(k):
        m = jnp.max(s, axis=0, keepdims=True)
        first = jnp.min(jnp.where(s == m, rows, float(n)), axis=0, keepdims=True)
        hit = rows == first
        vals.append(m)
        ids.append(first if payload is None else jnp.sum(jnp.where(hit, payload, 0.0), axis=0, keepdims=True))
        s = jnp.where(hit, -jnp.inf, s)
    return vals, ids


def _route_kernel(h_ref, wq_ref, keys_ref, idx_ref, gate_ref):
    hb = h_ref[...].astype(BF16)
    K = PEER_TOPK
    for h in range(PEER_HEADS):
        tops = []
        for half in range(2):
            c0 = (h * 2 + half) * PEER_HALF
            q = jnp.dot(hb, wq_ref[:, c0:c0 + PEER_HALF], preferred_element_type=F32)
            sT = lax.dot_general(keys_ref[h * 2 + half], q.astype(BF16), (((1,), (1,)), ((), ())),
                                 preferred_element_type=F32)
            tops.append(_top_rows(sT, K))
        (v1, i1), (v2, i2) = tops
        v2s = jnp.concatenate(v2, axis=0)
        i2s = jnp.concatenate(i2, axis=0)
        cand_s = jnp.concatenate([v1[a] + v2s for a in range(K)], axis=0)
        cand_e = jnp.concatenate([i1[a] * float(PEER_KEYS) + i2s for a in range(K)], axis=0)
        top_s, top_e = _top_rows(cand_s, K, payload=cand_e)
        ex = [jnp.exp(s - top_s[0]) for s in top_s]
        denom = ex[0]
        for e in ex[1:]:
            denom = denom + e
        inv = 1.0 / denom
        for j in range(K):
            r = h * K + j
            idx_ref[r:r + 1, :] = (top_e[j] * float(WORDS_PER_ROW)).astype(jnp.int32)
            gate_ref[r:r + 1, :] = ex[j] * inv


def _route(h2, wq, keys):
    T = h2.shape[0]
    full = lambda a: pl.BlockSpec(a.shape, lambda i: (0,) * a.ndim)
    out = pl.BlockSpec((PEER_SEL, ROUTE_TILE), lambda i: (0, i))
    return pl.pallas_call(
        _route_kernel,
        grid=(T // ROUTE_TILE,),
        in_specs=[pl.BlockSpec((ROUTE_TILE, D_MODEL), lambda i: (i, 0)), full(wq), full(keys)],
        out_specs=[out, out],
        out_shape=[jax.ShapeDtypeStruct((PEER_SEL, T), jnp.int32), jax.ShapeDtypeStruct((PEER_SEL, T), F32)],
        compiler_params=_params(1),
        name="route",
    )(h2, wq, keys)


def _gather_group(idx_ref, tbl_ref, tile_ref, g):
    S = TILE_STRIDE
    for q in range(GATHER_GROUP):
        for kk in range(PEER_SEL):
            i = pl.multiple_of(idx_ref[g * GATHER_GROUP + q, kk], WORDS_PER_ROW)
            tile_ref[pl.ds(q * WORDS_PER_ROW * S + kk, WORDS_PER_ROW, stride=S), :] = tbl_ref[pl.ds(i, WORDS_PER_ROW), :]


def _tile_chunk(tile_ref, q, j):
    S = TILE_STRIDE
    return pltpu.bitcast(tile_ref[pl.ds(q * WORDS_PER_ROW * S + j * S, PEER_SEL), :], BF16)


def _peer_u_kernel(idx_ref, x_ref, tbl_ref, o_ref, tile_ref):
    token_of_row = lax.broadcasted_iota(jnp.int32, (SUBLANES, 2 * PEER_SEL), 0) // 2

    def group(g, carry):
        _gather_group(idx_ref, tbl_ref, tile_ref, g)
        r0 = pl.multiple_of(SUBLANES * g, SUBLANES)
        x8 = x_ref[pl.ds(r0, SUBLANES), :].astype(BF16)
        out8 = jnp.zeros((SUBLANES, 2 * PEER_SEL), F32)
        for q in range(GATHER_GROUP):
            acc = jnp.zeros((SUBLANES, 2 * PEER_SEL), F32)
            for j in range(WORDS_PER_ROW):
                acc = acc + lax.dot_general(x8[:, j * LANES:(j + 1) * LANES], _tile_chunk(tile_ref, q, j),
                                            (((1,), (1,)), ((), ())), preferred_element_type=F32)
            out8 = jnp.where(token_of_row == q, acc, out8)
        o_ref[pl.ds(r0, SUBLANES), :] = out8
        return carry

    lax.fori_loop(0, GATHER_TILE // GATHER_GROUP, group, 0)


def _peer_v_kernel(idx_ref, w_ref, tbl_ref, o_ref, tile_ref):
    token_of_row = lax.broadcasted_iota(jnp.int32, (SUBLANES, LANES), 0) // 2

    def group(g, carry):
        _gather_group(idx_ref, tbl_ref, tile_ref, g)
        r0 = pl.multiple_of(SUBLANES * g, SUBLANES)
        w8 = w_ref[pl.ds(r0, SUBLANES), :].astype(BF16)
        for j in range(WORDS_PER_ROW):
            out8 = jnp.zeros((SUBLANES, LANES), F32)
            for q in range(GATHER_GROUP):
                r = jnp.dot(w8, _tile_chunk(tile_ref, q, j), preferred_element_type=F32)
                out8 = jnp.where(token_of_row == q, r, out8)
            o_ref[pl.ds(r0, SUBLANES), j * LANES:(j + 1) * LANES] = out8
        return carry

    lax.fori_loop(0, GATHER_TILE // GATHER_GROUP, group, 0)


def _peer_gather(kernel, name, idx, rows2, table, in_w, out_w):
    T = idx.shape[0]
    return pl.pallas_call(
        kernel,
        grid=(T // GATHER_TILE,),
        in_specs=[pl.BlockSpec((GATHER_TILE, PEER_SEL), lambda i: (i, 0), memory_space=pltpu.SMEM),
                  pl.BlockSpec((2 * GATHER_TILE, in_w), lambda i: (i, 0)),
                  pl.BlockSpec(memory_space=pltpu.VMEM)],
        out_specs=pl.BlockSpec((2 * GATHER_TILE, out_w), lambda i: (i, 0)),
        out_shape=jax.ShapeDtypeStruct((2 * T, out_w), F32),
        scratch_shapes=[pltpu.VMEM((GATHER_GROUP * WORDS_PER_ROW * TILE_STRIDE, LANES), jnp.int32)],
        compiler_params=pltpu.CompilerParams(dimension_semantics=("arbitrary",), vmem_limit_bytes=VMEM_LIMIT_TABLE),
        name=name,
    )(idx, rows2, table)


def _peer_w_kernel(a_ref, gate_ref, sel_ref, spread_ref, o_ref):
    act = jnp.dot(a_ref[...], sel_ref[...], precision=HIGHEST, preferred_element_type=F32)
    w = gate_ref[...] * (0.5 * act * (1.0 + lax.erf(act * (2.0 ** -0.5))))
    o_ref[...] = jnp.dot(w, spread_ref[...], precision=HIGHEST, preferred_element_type=F32)


def _peer_w(act_wide, gate):
    T = gate.shape[0]
    two = 2 * PEER_SEL
    k = np.arange(PEER_SEL)
    sel = np.zeros((2 * two, PEER_SEL), np.float32)
    sel[2 * k, k] = 1.0
    sel[two + 2 * k + 1, k] = 1.0
    spread = np.zeros((PEER_SEL, 2 * two), np.float32)
    spread[k, 2 * k] = 1.0
    spread[k, two + 2 * k + 1] = 1.0
    row = lambda w: pl.BlockSpec((ROW_TILE, w), lambda i: (i, 0))
    full = lambda a: pl.BlockSpec(a.shape, lambda i: (0,) * a.ndim)
    sel, spread = jnp.asarray(sel), jnp.asarray(spread)
    return pl.pallas_call(
        _peer_w_kernel,
        grid=(T // ROW_TILE,),
        in_specs=[row(2 * two), row(PEER_SEL), full(sel), full(spread)],
        out_specs=row(2 * two),
        out_shape=jax.ShapeDtypeStruct((T, 2 * two), F32),
        compiler_params=_params(1),
        name="peer_w",
    )(act_wide, gate, sel, spread)


def _final_kernel(h_ref, f_ref, g_ref, b_ref, o_ref):
    o_ref[...] = _layer_norm(ALPHA * h_ref[...] + f_ref[...], g_ref[...], b_ref[...])


def _final(h2, ffn, g, b):
    T = h2.shape[0]
    row = pl.BlockSpec((ROW_TILE, D_MODEL), lambda i: (i, 0))
    full = lambda a: pl.BlockSpec(a.shape, lambda i: (0,) * a.ndim)
    return pl.pallas_call(
        _final_kernel,
        grid=(T // ROW_TILE,),
        in_specs=[row, row, full(g), full(b)],
        out_specs=row,
        out_shape=jax.ShapeDtypeStruct((T, D_MODEL), F32),
        compiler_params=_params(1),
        name="final",
    )(h2, ffn, g, b)


def _pack_table(tab):
    bits = lax.bitcast_convert_type(tab.astype(BF16), jnp.uint16).astype(jnp.uint32)
    half = D_MODEL // 2
    words = bits[:, :half] | (bits[:, half:] << 16)
    return lax.bitcast_convert_type(words, jnp.int32).reshape(-1, LANES)


def _rope_tables(seq):
    half = MLA_ROPE // 2
    inv_freq = ROPE_THETA ** (-jnp.arange(half, dtype=F32) / half)
    ang = jnp.arange(seq, dtype=F32)[:, None] * inv_freq
    cos, sin = jnp.cos(ang), jnp.sin(ang)
    cos64 = jnp.concatenate([cos, cos], axis=1)
    sin64 = jnp.concatenate([-sin, sin], axis=1)
    return jnp.tile(cos64, (1, MLA_HEADS)), jnp.tile(sin64, (1, MLA_HEADS))


def _swap_halves(w):
    half = w.shape[-1] // 2
    return jnp.concatenate([w[..., half:], w[..., :half]], axis=-1)


def kernel(x, ln_in_g, ln_in_b, w_in, conv_w, conv_b, dt_bias, a_log, d_skip, ssm_norm_g, q_norm_g, w_q_b, kv_norm_g, w_kv_b, w_out, ln_mix_g, ln_mix_b, w_query, sub_keys, u_table, v_table, ln_ffn_g, ln_ffn_b):
    batch, seq, d = x.shape
    assert d == D_MODEL and w_in.shape[0] == DEPTH == 1
    assert seq % ROW_TILE == 0 and seq % SSM_CHUNK == 0 and seq % ATTN_BLOCK == 0
    T = batch * seq
    assert T % GATHER_TILE == 0 and T % ROUTE_TILE == 0
    x2 = x.reshape(T, d)
    vec = lambda v: v.reshape(1, -1)

    wi = w_in[0]
    o = np.cumsum((0, SSM_WIDTH, SSM_CONV_WIDTH, SSM_HEADS, MLA_Q_RANK, MLA_KV_RANK, MLA_ROPE))
    w_kpe = wi[:, o[5]:o[6]]
    w_packed = jnp.concatenate(
        [wi[:, o[0]:o[2]], wi[:, o[3]:o[5]], w_kpe, _swap_halves(w_kpe), wi[:, o[2]:o[3]],
         jnp.zeros((d, LANES - SSM_HEADS), F32)], axis=1).astype(BF16)
    z, xbc, ql, kvl, kpe, dt = _inproj(x2, vec(ln_in_g), vec(ln_in_b), w_packed)

    dtT = jnp.transpose(dt[:, :SSM_HEADS].reshape(batch, seq, SSM_HEADS), (0, 2, 1))
    y_ssm = _ssd(z, xbc, dt, dtT, conv_w[0], conv_b[0], dt_bias[0], a_log[0], d_skip[0], ssm_norm_g[0], batch, seq)

    wq = w_q_b[0].reshape(MLA_Q_RANK, MLA_HEADS, MLA_NOPE + MLA_ROPE)
    wq_pe = wq[:, :, MLA_NOPE:]
    wkv = w_kv_b[0].reshape(MLA_KV_RANK, MLA_HEADS, MLA_NOPE + MLA_V)
    flat = lambda w: w.reshape(w.shape[0], -1).astype(BF16)
    cosq, sinq = _rope_tables(seq)
    q, k, v = _mlaproj(ql, kvl, kpe, cosq, sinq, q_norm_g[0], kv_norm_g[0], flat(wq[:, :, :MLA_NOPE]), flat(wq_pe),
                       flat(_swap_halves(wq_pe)), flat(wkv[:, :, :MLA_NOPE]), flat(wkv[:, :, MLA_NOPE:]), batch, seq)
    y_mla = _flash(q, k, v).reshape(T, MLA_WIDTH)

    wo = w_out[0].astype(BF16)
    h2 = _outproj(x2, y_ssm, y_mla, vec(ln_in_g), vec(ln_in_b), wo[:SSM_WIDTH], wo[SSM_WIDTH:],
                  vec(ln_mix_g[0]), vec(ln_mix_b[0]))

    keys = sub_keys[0].reshape(PEER_HEADS * 2, PEER_KEYS, PEER_HALF).astype(BF16)
    idxT, gateT = _route(h2, w_query[0].astype(BF16), keys)
    idx = jnp.transpose(idxT)
    gate = jnp.transpose(gateT)

    half = D_MODEL // 2
    act_wide = _peer_gather(_peer_u_kernel, "peer_u", idx, h2.reshape(2 * T, half), _pack_table(u_table[0]),
                            half, 2 * PEER_SEL)
    w_wide = _peer_w(act_wide.reshape(T, 4 * PEER_SEL), gate)
    ffn = _peer_gather(_peer_v_kernel, "peer_v", idx, w_wide.reshape(2 * T, 2 * PEER_SEL), _pack_table(v_table[0]),
                       2 * PEER_SEL, half)
    out = _final(h2, ffn.reshape(T, D_MODEL), vec(ln_ffn_g[0]), vec(ln_ffn_b[0]))
    return out.reshape(batch, seq, d)
```

```python
import functools

import numpy as np
import jax
import jax.numpy as jnp
from jax import lax
from jax.experimental import pallas as pl
from jax.experimental.pallas import tpu as pltpu

F32 = jnp.float32
BF16 = jnp.bfloat16
HIGHEST = lax.Precision.HIGHEST

D_MODEL = 1024
SSM_HEADS = 16
SSM_HEAD_DIM = 64
SSM_WIDTH = SSM_HEADS * SSM_HEAD_DIM
SSM_GROUPS = 2
SSM_STATE = 128
SSM_CONV = 4
SSM_CHUNK = 256
SSM_CONV_WIDTH = SSM_WIDTH + 2 * SSM_GROUPS * SSM_STATE
MLA_HEADS = 8
MLA_Q_RANK = 384
MLA_KV_RANK = 256
MLA_NOPE = 128
MLA_ROPE = 64
MLA_V = 128
MLA_WIDTH = MLA_HEADS * MLA_V
ROPE_THETA = 10000.0
PEER_HEADS = 8
PEER_KEYS = 128
PEER_HALF = 128
PEER_TOPK = 16
PEER_SEL = PEER_HEADS * PEER_TOPK
DEPTH = 1
ALPHA = (2.0 * DEPTH) ** 0.25
EPS = 1e-5

LANES = 128
SUBLANES = 8
MXU_DEPTH = 256
VMEM_LIMIT = 48 * 1024 * 1024
VMEM_LIMIT_TABLE = 56 * 1024 * 1024

ROW_TILE = 512
ATTN_BLOCK = 512
ATTN_Q_TILE = 2 * ATTN_BLOCK
ROUTE_TILE = 128
GATHER_TILE = 256
GATHER_GROUP = SUBLANES
TILE_STRIDE = PEER_SEL + 1
WORDS_PER_ROW = D_MODEL // 2 // LANES

_OFF_Z = 0
_OFF_XBC = _OFF_Z + SSM_WIDTH
_OFF_QL = _OFF_XBC + SSM_CONV_WIDTH
_OFF_KVL = _OFF_QL + MLA_Q_RANK
_OFF_KPE = _OFF_KVL + MLA_KV_RANK
_OFF_DT = _OFF_KPE + LANES
_IN_COLS = _OFF_DT + LANES


def _params(n_grid):
    return pltpu.CompilerParams(dimension_semantics=("arbitrary",) * n_grid, vmem_limit_bytes=VMEM_LIMIT)


def _layer_norm(x, g, b):
    mu = jnp.mean(x, axis=-1, keepdims=True)
    xc = x - mu
    var = jnp.mean(xc * xc, axis=-1, keepdims=True)
    return xc * lax.rsqrt(var + EPS) * g + b


def _rms_norm(x, g):
    return x * lax.rsqrt(jnp.mean(x * x, axis=-1, keepdims=True) + EPS) * g


def _sigmoid(x):
    return 1.0 / (1.0 + jnp.exp(-x))


def _softplus(x):
    return jnp.maximum(x, 0.0) + jnp.log1p(jnp.exp(-jnp.abs(x)))


def _inproj_kernel(x_ref, g_ref, b_ref, w_ref, z_ref, xbc_ref, ql_ref, kvl_ref, kpe_ref, dt_ref):
    h = _layer_norm(x_ref[...], g_ref[...], b_ref[...]).astype(BF16)

    def seg(lo, hi):
        return jnp.dot(h, w_ref[:, lo:hi], preferred_element_type=F32)

    z_ref[...] = seg(_OFF_Z, _OFF_XBC)
    xbc_ref[...] = seg(_OFF_XBC, _OFF_QL)
    ql_ref[...] = seg(_OFF_QL, _OFF_KVL)
    kvl_ref[...] = seg(_OFF_KVL, _OFF_KPE)
    kpe_ref[...] = seg(_OFF_KPE, _OFF_DT)
    dt_ref[...] = seg(_OFF_DT, _IN_COLS)


def _inproj(x2, ln_g, ln_b, w_packed):
    T = x2.shape[0]
    widths = (SSM_WIDTH, SSM_CONV_WIDTH, MLA_Q_RANK, MLA_KV_RANK, LANES, LANES)
    row = lambda w: pl.BlockSpec((ROW_TILE, w), lambda i: (i, 0))
    full = lambda a: pl.BlockSpec(a.shape, lambda i: (0,) * a.ndim)
    return pl.pallas_call(
        _inproj_kernel,
        grid=(T // ROW_TILE,),
        in_specs=[row(D_MODEL), full(ln_g), full(ln_b), full(w_packed)],
        out_specs=[row(w) for w in widths],
        out_shape=[jax.ShapeDtypeStruct((T, w), F32) for w in widths],
        compiler_params=_params(1),
        name="inproj",
    )(x2, ln_g, ln_b, w_packed)


def _ssd_kernel(z_ref, xbc_ref, dt_ref, dtT_ref, convw_ref, convb_ref, dtb_ref, dtbT_ref, alog_ref, alogT_ref,
                dskip_ref, g_ref, e_ref, y_ref, xpad_ref, st_ref):
    Q = SSM_CHUNK
    HALO = SUBLANES
    c = pl.program_id(1)

    @pl.when(c == 0)
    def _():
        xpad_ref[0:HALO, :] = jnp.zeros((HALO, SSM_CONV_WIDTH), F32)
        st_ref[...] = jnp.zeros(st_ref.shape, F32)

    raw = xbc_ref[...]
    xpad_ref[HALO:HALO + Q, :] = raw
    conv = convb_ref[...] + raw * convw_ref[SSM_CONV - 1:SSM_CONV, :]
    for k in range(SSM_CONV - 1):
        off = HALO - (SSM_CONV - 1) + k
        conv = conv + xpad_ref[off:off + Q, :] * convw_ref[k:k + 1, :]
    xpad_ref[0:HALO, :] = raw[Q - HALO:Q, :]
    xbc = conv * _sigmoid(conv)
    xs = xbc[:, :SSM_WIDTH]
    bm = xbc[:, SSM_WIDTH:SSM_WIDTH + SSM_GROUPS * SSM_STATE]
    cm = xbc[:, SSM_WIDTH + SSM_GROUPS * SSM_STATE:]

    dt = _softplus(dt_ref[...] + dtb_ref[...])
    dtT = _softplus(dtT_ref[...] + dtbT_ref[...])
    a = dt * (-jnp.exp(alog_ref[...]))
    aT = dtT * (-jnp.exp(alogT_ref[...]))
    row = lax.broadcasted_iota(jnp.int32, (Q, Q), 0)
    col = lax.broadcasted_iota(jnp.int32, (Q, Q), 1)
    causal = col <= row
    lower = causal.astype(F32)
    upper = (row <= col).astype(F32)
    a_cum = jnp.dot(lower, a, precision=HIGHEST, preferred_element_type=F32)
    a_cumT = jnp.dot(aT, upper, precision=HIGHEST, preferred_element_type=F32)
    a_last = a_cum[Q - 1:Q, :]

    expand = e_ref[...]
    ex = lambda v: jnp.dot(v, expand, precision=HIGHEST, preferred_element_type=F32)
    xdt = xs * ex(dt)
    decay_out = ex(jnp.exp(a_cum))
    decay_st = ex(jnp.exp(a_last - a_cum))
    chunk_decay = ex(jnp.broadcast_to(jnp.exp(a_last), (SUBLANES, LANES)))[0:1, :]
    xdt_b = xdt.astype(BF16)
    wx_b = (decay_st * xdt).astype(BF16)

    heads_per_group = SSM_HEADS // SSM_GROUPS
    gw = heads_per_group * SSM_HEAD_DIM
    y_parts = []
    for g in range(SSM_GROUPS):
        bg = bm[:, g * SSM_STATE:(g + 1) * SSM_STATE]
        cg = cm[:, g * SSM_STATE:(g + 1) * SSM_STATE].astype(BF16)
        cb = lax.dot_general(cg, bg.astype(BF16), (((1,), (1,)), ((), ())), preferred_element_type=F32)
        yd = []
        for e in range(heads_per_group):
            h = g * heads_per_group + e
            seg = a_cum[:, h:h + 1] - a_cumT[h:h + 1, :]
            m = (cb * jnp.exp(jnp.where(causal, seg, -jnp.inf))).astype(BF16)
            yd.append(jnp.dot(m, xdt_b[:, h * SSM_HEAD_DIM:(h + 1) * SSM_HEAD_DIM], preferred_element_type=F32))
        yd = jnp.concatenate(yd, axis=1)
        st = st_ref[g]
        y_off = jnp.dot(cg, st.astype(BF16), preferred_element_type=F32) * decay_out[:, g * gw:(g + 1) * gw]
        y_parts.append(yd + y_off)
        bgT = jnp.transpose(bg).astype(BF16)
        st_ref[g] = st * chunk_decay[:, g * gw:(g + 1) * gw] + jnp.dot(
            bgT, wx_b[:, g * gw:(g + 1) * gw], preferred_element_type=F32)
    y = jnp.concatenate(y_parts, axis=1)
    d_e = ex(jnp.broadcast_to(dskip_ref[...], (SUBLANES, LANES)))[0:1, :]
    y = y + d_e * xs
    zz = z_ref[...]
    y = y * (zz * _sigmoid(zz))
    y_ref[...] = _rms_norm(y, g_ref[...]).astype(BF16)


def _ssd(z, xbc, dt, dtT, conv_w, conv_b, dt_bias, a_log, d_skip, norm_g, batch, seq):
    T = z.shape[0]
    nc = seq // SSM_CHUNK
    pad = lambda v: jnp.pad(v.reshape(1, -1), ((0, 0), (0, LANES - v.shape[-1])))
    colv = lambda v: v.reshape(-1, 1)
    head_of_lane = np.arange(SSM_WIDTH) // SSM_HEAD_DIM
    expand = jnp.asarray((np.arange(LANES)[:, None] == head_of_lane[None, :]).astype(np.float32))
    row = lambda w: pl.BlockSpec((SSM_CHUNK, w), lambda b, c: (b * nc + c, 0))
    full = lambda a: pl.BlockSpec(a.shape, lambda b, c: (0,) * a.ndim)
    args = (z, xbc, dt, dtT, conv_w, conv_b.reshape(1, -1), pad(dt_bias), colv(dt_bias), pad(a_log), colv(a_log),
            pad(d_skip), norm_g.reshape(1, -1), expand)
    in_specs = [row(SSM_WIDTH), row(SSM_CONV_WIDTH), row(LANES),
                pl.BlockSpec((None, SSM_HEADS, SSM_CHUNK), lambda b, c: (b, 0, c))]
    in_specs += [full(a) for a in args[4:]]
    return pl.pallas_call(
        _ssd_kernel,
        grid=(batch, nc),
        in_specs=in_specs,
        out_specs=row(SSM_WIDTH),
        out_shape=jax.ShapeDtypeStruct((T, SSM_WIDTH), BF16),
        scratch_shapes=[pltpu.VMEM((SUBLANES + SSM_CHUNK, SSM_CONV_WIDTH), F32),
                        pltpu.VMEM((SSM_GROUPS, SSM_STATE, SSM_WIDTH // SSM_GROUPS), F32)],
        compiler_params=_params(2),
        name="ssd",
    )(*args)


def _mlaproj_kernel(ql_ref, kvl_ref, kpe_ref, cos_ref, sin_ref, qg_ref, kvg_ref, wqn_ref, wqp_ref, wqs_ref,
                    wkn_ref, wv_ref, q_ref, k_ref, v_ref):
    scale = (MLA_NOPE + MLA_ROPE) ** -0.5
    cos = cos_ref[...]
    sin = sin_ref[...]
    qn = _rms_norm(ql_ref[...], qg_ref[...]).astype(BF16)
    q_nope = jnp.dot(qn, wqn_ref[...], preferred_element_type=F32) * scale
    q_pe = (jnp.dot(qn, wqp_ref[...], preferred_element_type=F32) * cos
            + jnp.dot(qn, wqs_ref[...], preferred_element_type=F32) * sin) * scale
    kvn = _rms_norm(kvl_ref[...], kvg_ref[...]).astype(BF16)
    k_nope = jnp.dot(kvn, wkn_ref[...], preferred_element_type=F32)
    v = jnp.dot(kvn, wv_ref[...], preferred_element_type=F32)
    kpe = kpe_ref[...]
    k_pe = kpe[:, :MLA_ROPE] * cos[:, :MLA_ROPE] + kpe[:, MLA_ROPE:] * sin[:, :MLA_ROPE]
    zpad = jnp.zeros((q_nope.shape[0], MXU_DEPTH - MLA_NOPE - MLA_ROPE), F32)
    for h in range(MLA_HEADS):
        q_ref[h] = jnp.concatenate(
            [q_nope[:, h * MLA_NOPE:(h + 1) * MLA_NOPE], q_pe[:, h * MLA_ROPE:(h + 1) * MLA_ROPE], zpad],
            axis=1).astype(BF16)
        k_ref[h] = jnp.concatenate([k_nope[:, h * MLA_NOPE:(h + 1) * MLA_NOPE], k_pe, zpad], axis=1).astype(BF16)
        v_ref[h] = v[:, h * MLA_V:(h + 1) * MLA_V].astype(BF16)


def _mlaproj(ql, kvl, kpe, cosq, sinq, q_g, kv_g, wqn, wqp, wqs, wkn, wv, batch, seq):
    T = ql.shape[0]
    per_seq = seq // ROW_TILE
    row = lambda w: pl.BlockSpec((ROW_TILE, w), lambda i: (i, 0))
    pos = lambda w: pl.BlockSpec((ROW_TILE, w), lambda i: (i % per_seq, 0))
    full = lambda a: pl.BlockSpec(a.shape, lambda i: (0,) * a.ndim)
    head = lambda w: pl.BlockSpec((None, MLA_HEADS, ROW_TILE, w), lambda i: (i // per_seq, 0, i % per_seq, 0))
    consts = (q_g.reshape(1, -1), kv_g.reshape(1, -1), wqn, wqp, wqs, wkn, wv)
    return pl.pallas_call(
        _mlaproj_kernel,
        grid=(T // ROW_TILE,),
        in_specs=[row(MLA_Q_RANK), row(MLA_KV_RANK), row(LANES), pos(MLA_HEADS * MLA_ROPE), pos(MLA_HEADS * MLA_ROPE)]
        + [full(a) for a in consts],
        out_specs=[head(MXU_DEPTH), head(MXU_DEPTH), head(MLA_V)],
        out_shape=[jax.ShapeDtypeStruct((batch, MLA_HEADS, seq, MXU_DEPTH), BF16),
                   jax.ShapeDtypeStruct((batch, MLA_HEADS, seq, MXU_DEPTH), BF16),
                   jax.ShapeDtypeStruct((batch, MLA_HEADS, seq, MLA_V), BF16)],
        compiler_params=_params(1),
        name="mlaproj",
    )(ql, kvl, kpe, cosq, sinq, *consts)


def _flash_kernel(q_ref, k_ref, v_ref, o_ref, m_ref, l_ref, acc_ref):
    blk = ATTN_BLOCK
    halves = ATTN_Q_TILE // blk
    qi = pl.program_id(2)
    m_ref[...] = jnp.full(m_ref.shape, -jnp.inf, F32)
    l_ref[...] = jnp.zeros(l_ref.shape, F32)
    acc_ref[...] = jnp.zeros(acc_ref.shape, F32)
    below_diag = (lax.broadcasted_iota(jnp.int32, (blk, blk), 1) <= lax.broadcasted_iota(jnp.int32, (blk, blk), 0))

    def step(half, kv_block, masked):
        rows = pl.ds(half * blk, blk)
        off = pl.multiple_of(kv_block * blk, blk)
        s = lax.dot_general(q_ref[rows, :], k_ref[pl.ds(off, blk), :], (((1,), (1,)), ((), ())),
                            preferred_element_type=F32)
        if masked:
            s = jnp.where(below_diag, s, -jnp.inf)
        m_prev = m_ref[rows, :]
        m_new = jnp.maximum(m_prev, jnp.max(s, axis=-1, keepdims=True))
        p = jnp.exp(s - m_new)
        alpha = jnp.exp(m_prev - m_new)
        l_ref[rows, :] = alpha * l_ref[rows, :] + jnp.sum(p, axis=-1, keepdims=True)
        acc_ref[rows, :] = alpha * acc_ref[rows, :] + jnp.dot(p.astype(BF16), v_ref[pl.ds(off, blk), :],
                                                              preferred_element_type=F32)
        m_ref[rows, :] = m_new

    def bulk(j, carry):
        for half in range(halves):
            step(half, j, False)
        return carry

    first = qi * halves
    lax.fori_loop(0, first, bulk, 0)
    for half in range(halves):
        for d in range(half):
            step(half, first + d, False)
        step(half, first + half, True)
    o_ref[...] = (acc_ref[...] / l_ref[...]).astype(BF16)


def _flash(q, k, v):
    batch, heads, seq, _ = q.shape
    qspec = pl.BlockSpec((None, None, ATTN_Q_TILE, MXU_DEPTH), lambda b, h, i: (b, h, i, 0))
    kspec = pl.BlockSpec((None, None, seq, MXU_DEPTH), lambda b, h, i: (b, h, 0, 0))
    vspec = pl.BlockSpec((None, None, seq, MLA_V), lambda b, h, i: (b, h, 0, 0))
    return pl.pallas_call(
        _flash_kernel,
        grid=(batch, heads, seq // ATTN_Q_TILE),
        in_specs=[qspec, kspec, vspec],
        out_specs=pl.BlockSpec((None, ATTN_Q_TILE, MLA_V), lambda b, h, i: (b, i, h)),
        out_shape=jax.ShapeDtypeStruct((batch, seq, heads * MLA_V), BF16),
        scratch_shapes=[pltpu.VMEM((ATTN_Q_TILE, 1), F32), pltpu.VMEM((ATTN_Q_TILE, 1), F32),
                        pltpu.VMEM((ATTN_Q_TILE, MLA_V), F32)],
        compiler_params=_params(3),
        name="flash",
    )(q, k, v)


def _outproj_kernel(x_ref, ys_ref, ym_ref, g0_ref, b0_ref, wa_ref, wb_ref, g1_ref, b1_ref, o_ref):
    h = _layer_norm(x_ref[...], g0_ref[...], b0_ref[...])
    mix = (jnp.dot(ys_ref[...], wa_ref[...], preferred_element_type=F32)
           + jnp.dot(ym_ref[...], wb_ref[...], preferred_element_type=F32))
    o_ref[...] = _layer_norm(ALPHA * h + mix, g1_ref[...], b1_ref[...])


def _outproj(x2, y_ssm, y_mla, g0, b0, wa, wb, g1, b1):
    T = x2.shape[0]
    row = lambda w: pl.BlockSpec((ROW_TILE, w), lambda i: (i, 0))
    full = lambda a: pl.BlockSpec(a.shape, lambda i: (0,) * a.ndim)
    consts = (g0, b0, wa, wb, g1, b1)
    return pl.pallas_call(
        _outproj_kernel,
        grid=(T // ROW_TILE,),
        in_specs=[row(D_MODEL), row(SSM_WIDTH), row(MLA_WIDTH)] + [full(a) for a in consts],
        out_specs=row(D_MODEL),
        out_shape=jax.ShapeDtypeStruct((T, D_MODEL), F32),
        compiler_params=_params(1),
        name="outproj",
    )(x2, y_ssm, y_mla, *consts)


def _top_rows(s, k, payload=None):
    n = s.shape[0]
    rows = lax.broadcasted_iota(jnp.int32, s.shape, 0).astype(F32)
    vals, ids = [], []
    for _ in range(k):
        m = jnp.max(s, axis=0, keepdims=True)
        first = jnp.min(jnp.where(s == m, rows, float(n)), axis=0, keepdims=True)
        hit = rows == first
        vals.append(m)
        ids.append(first if payload is None else jnp.sum(jnp.where(hit, payload, 0.0), axis=0, keepdims=True))
        s = jnp.where(hit, -jnp.inf, s)
    return vals, ids


def _pair_candidates(v1, v2, combine):
    K = PEER_TOPK
    v1s = jnp.concatenate(v1, axis=0)
    v2s = jnp.concatenate(v2, axis=0)
    pieces = [combine(v1[0], v2s)]
    pieces += [combine(v1[a], v2s[:SUBLANES]) for a in range(1, SUBLANES)]
    pieces.append(combine(v1s[SUBLANES:], v2[0]))
    return jnp.concatenate(pieces, axis=0)


def _route_kernel(h_ref, wq_ref, keys_ref, idx_ref, gate_ref, off_scr, gate_scr):
    hb = h_ref[...].astype(BF16)
    K = PEER_TOPK
    for h in range(PEER_HEADS):
        tops = []
        for half in range(2):
            c0 = (h * 2 + half) * PEER_HALF
            q = jnp.dot(hb, wq_ref[:, c0:c0 + PEER_HALF], preferred_element_type=F32)
            sT = lax.dot_general(keys_ref[h * 2 + half], q.astype(BF16), (((1,), (1,)), ((), ())),
                                 preferred_element_type=F32)
            tops.append(_top_rows(sT, K))
        (v1, i1), (v2, i2) = tops
        cand_s = _pair_candidates(v1, v2, lambda x, y: x + y)
        cand_e = _pair_candidates(i1, i2, lambda x, y: x * float(PEER_KEYS) + y)
        top_s, top_e = _top_rows(cand_s, K, payload=cand_e)
        ex = [jnp.exp(s - top_s[0]) for s in top_s]
        denom = ex[0]
        for e in ex[1:]:
            denom = denom + e
        inv = 1.0 / denom
        for j in range(K):
            r = h * K + j
            off_scr[r:r + 1, :] = top_e[j] * float(WORDS_PER_ROW)
            gate_scr[r:r + 1, :] = ex[j] * inv
    idx_ref[...] = jnp.transpose(off_scr[...]).astype(jnp.int32)
    gate_ref[...] = jnp.transpose(gate_scr[...])


def _route(h2, wq, keys):
    T = h2.shape[0]
    full = lambda a: pl.BlockSpec(a.shape, lambda i: (0,) * a.ndim)
    out = pl.BlockSpec((ROUTE_TILE, PEER_SEL), lambda i: (i, 0))
    return pl.pallas_call(
        _route_kernel,
        grid=(T // ROUTE_TILE,),
        in_specs=[pl.BlockSpec((ROUTE_TILE, D_MODEL), lambda i: (i, 0)), full(wq), full(keys)],
        out_specs=[out, out],
        out_shape=[jax.ShapeDtypeStruct((T, PEER_SEL), jnp.int32), jax.ShapeDtypeStruct((T, PEER_SEL), F32)],
        scratch_shapes=[pltpu.VMEM((PEER_SEL, ROUTE_TILE), F32), pltpu.VMEM((PEER_SEL, ROUTE_TILE), F32)],
        compiler_params=_params(1),
        name="route",
    )(h2, wq, keys)


def _gather_group(idx_ref, tbl_ref, tile_ref, g):
    S = TILE_STRIDE
    for q in range(GATHER_GROUP):
        for kk in range(PEER_SEL):
            i = pl.multiple_of(idx_ref[g * GATHER_GROUP + q, kk], WORDS_PER_ROW)
            tile_ref[pl.ds(q * WORDS_PER_ROW * S + kk, WORDS_PER_ROW, stride=S), :] = tbl_ref[pl.ds(i, WORDS_PER_ROW), :]


def _tile_chunk(tile_ref, q, j):
    S = TILE_STRIDE
    return pltpu.bitcast(tile_ref[pl.ds(q * WORDS_PER_ROW * S + j * S, PEER_SEL), :], BF16)


def _pipelined_groups(idx_ref, tbl_ref, tile_a, tile_b, contract):
    groups = GATHER_TILE // GATHER_GROUP
    _gather_group(idx_ref, tbl_ref, tile_a, 0)

    def pair(p, carry):
        g = 2 * p
        _gather_group(idx_ref, tbl_ref, tile_b, g + 1)
        contract(tile_a, g)
        _gather_group(idx_ref, tbl_ref, tile_a, jnp.minimum(g + 2, groups - 1))
        contract(tile_b, g + 1)
        return carry

    lax.fori_loop(0, groups // 2, pair, 0)


def _keep_own_rows(acc, q, lo, hi):
    own = lax.broadcasted_iota(jnp.int32, lo.shape, 0) == q
    return jnp.where(own, acc[:GATHER_GROUP], lo), jnp.where(own, acc[GATHER_GROUP:], hi)


def _peer_u_kernel(idx_ref, x_ref, tbl_ref, lo_ref, hi_ref, tile_a, tile_b):
    half = D_MODEL // 2

    def contract(tile_ref, g):
        r0 = pl.multiple_of(GATHER_GROUP * g, GATHER_GROUP)
        x = x_ref[pl.ds(r0, GATHER_GROUP), :]
        lhs = jnp.concatenate([x[:, :half], x[:, half:]], axis=0).astype(BF16)
        lo = jnp.zeros((GATHER_GROUP, 2 * PEER_SEL), F32)
        hi = jnp.zeros((GATHER_GROUP, 2 * PEER_SEL), F32)
        for q in range(GATHER_GROUP):
            acc = jnp.zeros((2 * GATHER_GROUP, 2 * PEER_SEL), F32)
            for j in range(WORDS_PER_ROW):
                acc = acc + lax.dot_general(lhs[:, j * LANES:(j + 1) * LANES], _tile_chunk(tile_ref, q, j),
                                            (((1,), (1,)), ((), ())), preferred_element_type=F32)
            lo, hi = _keep_own_rows(acc, q, lo, hi)
        lo_ref[pl.ds(r0, GATHER_GROUP), :] = lo
        hi_ref[pl.ds(r0, GATHER_GROUP), :] = hi

    _pipelined_groups(idx_ref, tbl_ref, tile_a, tile_b, contract)


def _peer_v_kernel(idx_ref, wlo_ref, whi_ref, tbl_ref, o_ref, tile_a, tile_b):
    half = D_MODEL // 2

    def contract(tile_ref, g):
        r0 = pl.multiple_of(GATHER_GROUP * g, GATHER_GROUP)
        rows = pl.ds(r0, GATHER_GROUP)
        lhs = jnp.concatenate([wlo_ref[rows, :], whi_ref[rows, :]], axis=0).astype(BF16)
        for j in range(WORDS_PER_ROW):
            lo = jnp.zeros((GATHER_GROUP, LANES), F32)
            hi = jnp.zeros((GATHER_GROUP, LANES), F32)
            for q in range(GATHER_GROUP):
                r = jnp.dot(lhs, _tile_chunk(tile_ref, q, j), preferred_element_type=F32)
                lo, hi = _keep_own_rows(r, q, lo, hi)
            o_ref[rows, j * LANES:(j + 1) * LANES] = lo
            o_ref[rows, half + j * LANES:half + (j + 1) * LANES] = hi

    _pipelined_groups(idx_ref, tbl_ref, tile_a, tile_b, contract)


def _peer_call(kernel, name, idx, row_inputs, table, out_widths):
    T = idx.shape[0]
    tile = pltpu.VMEM((GATHER_GROUP * WORDS_PER_ROW * TILE_STRIDE, LANES), jnp.int32)
    row = lambda w: pl.BlockSpec((GATHER_TILE, w), lambda i: (i, 0))
    return pl.pallas_call(
        kernel,
        grid=(T // GATHER_TILE,),
        in_specs=[pl.BlockSpec((GATHER_TILE, PEER_SEL), lambda i: (i, 0), memory_space=pltpu.SMEM)]
        + [row(a.shape[1]) for a in row_inputs] + [pl.BlockSpec(memory_space=pltpu.VMEM)],
        out_specs=[row(w) for w in out_widths],
        out_shape=[jax.ShapeDtypeStruct((T, w), F32) for w in out_widths],
        scratch_shapes=[tile, tile],
        compiler_params=pltpu.CompilerParams(dimension_semantics=("arbitrary",), vmem_limit_bytes=VMEM_LIMIT_TABLE),
        name=name,
    )(idx, *row_inputs, table)


def _peer_w_kernel(lo_ref, hi_ref, gate_ref, sel_e_ref, sel_o_ref, wlo_ref, whi_ref):
    dims = (((1,), (1,)), ((), ()))
    act = (lax.dot_general(lo_ref[...], sel_e_ref[...], dims, precision=HIGHEST, preferred_element_type=F32)
           + lax.dot_general(hi_ref[...], sel_o_ref[...], dims, precision=HIGHEST, preferred_element_type=F32))
    w = gate_ref[...] * (0.5 * act * (1.0 + lax.erf(act * (2.0 ** -0.5))))
    wlo_ref[...] = jnp.dot(w, sel_e_ref[...], precision=HIGHEST, preferred_element_type=F32)
    whi_ref[...] = jnp.dot(w, sel_o_ref[...], precision=HIGHEST, preferred_element_type=F32)


def _peer_w(act_lo, act_hi, gate):
    T = gate.shape[0]
    k = np.arange(PEER_SEL)
    sel_e = np.zeros((PEER_SEL, 2 * PEER_SEL), np.float32)
    sel_e[k, 2 * k] = 1.0
    sel_o = np.zeros((PEER_SEL, 2 * PEER_SEL), np.float32)
    sel_o[k, 2 * k + 1] = 1.0
    sel_e, sel_o = jnp.asarray(sel_e), jnp.asarray(sel_o)
    row = lambda w: pl.BlockSpec((ROW_TILE, w), lambda i: (i, 0))
    full = lambda a: pl.BlockSpec(a.shape, lambda i: (0,) * a.ndim)
    wide = 2 * PEER_SEL
    return pl.pallas_call(
        _peer_w_kernel,
        grid=(T // ROW_TILE,),
        in_specs=[row(wide), row(wide), row(PEER_SEL), full(sel_e), full(sel_o)],
        out_specs=[row(wide), row(wide)],
        out_shape=[jax.ShapeDtypeStruct((T, wide), F32), jax.ShapeDtypeStruct((T, wide), F32)],
        compiler_params=_params(1),
        name="peer_w",
    )(act_lo, act_hi, gate, sel_e, sel_o)


def _final_kernel(h_ref, f_ref, g_ref, b_ref, o_ref):
    o_ref[...] = _layer_norm(ALPHA * h_ref[...] + f_ref[...], g_ref[...], b_ref[...])


def _final(h2, ffn, g, b):
    T = h2.shape[0]
    row = pl.BlockSpec((ROW_TILE, D_MODEL), lambda i: (i, 0))
    full = lambda a: pl.BlockSpec(a.shape, lambda i: (0,) * a.ndim)
    return pl.pallas_call(
        _final_kernel,
        grid=(T // ROW_TILE,),
        in_specs=[row, row, full(g), full(b)],
        out_specs=row,
        out_shape=jax.ShapeDtypeStruct((T, D_MODEL), F32),
        compiler_params=_params(1),
        name="final",
    )(h2, ffn, g, b)


def _pack_table(tab):
    bits = lax.bitcast_convert_type(tab.astype(BF16), jnp.uint16).astype(jnp.uint32)
    half = D_MODEL // 2
    words = bits[:, :half] | (bits[:, half:] << 16)
    return lax.bitcast_convert_type(words, jnp.int32).reshape(-1, LANES)


def _rope_tables(seq):
    half = MLA_ROPE // 2
    inv_freq = ROPE_THETA ** (-jnp.arange(half, dtype=F32) / half)
    ang = jnp.arange(seq, dtype=F32)[:, None] * inv_freq
    cos, sin = jnp.cos(ang), jnp.sin(ang)
    cos64 = jnp.concatenate([cos, cos], axis=1)
    sin64 = jnp.concatenate([-sin, sin], axis=1)
    return jnp.tile(cos64, (1, MLA_HEADS)), jnp.tile(sin64, (1, MLA_HEADS))


def _swap_halves(w):
    half = w.shape[-1] // 2
    return jnp.concatenate([w[..., half:], w[..., :half]], axis=-1)


def kernel(x, ln_in_g, ln_in_b, w_in, conv_w, conv_b, dt_bias, a_log, d_skip, ssm_norm_g, q_norm_g, w_q_b, kv_norm_g, w_kv_b, w_out, ln_mix_g, ln_mix_b, w_query, sub_keys, u_table, v_table, ln_ffn_g, ln_ffn_b):
    batch, seq, d = x.shape
    assert d == D_MODEL and w_in.shape[0] == DEPTH == 1
    assert seq % ROW_TILE == 0 and seq % SSM_CHUNK == 0 and seq % ATTN_Q_TILE == 0
    T = batch * seq
    assert T % GATHER_TILE == 0 and T % ROUTE_TILE == 0
    x2 = x.reshape(T, d)
    vec = lambda v: v.reshape(1, -1)

    wi = w_in[0]
    o = np.cumsum((0, SSM_WIDTH, SSM_CONV_WIDTH, SSM_HEADS, MLA_Q_RANK, MLA_KV_RANK, MLA_ROPE))
    w_kpe = wi[:, o[5]:o[6]]
    w_packed = jnp.concatenate(
        [wi[:, o[0]:o[2]], wi[:, o[3]:o[5]], w_kpe, _swap_halves(w_kpe), wi[:, o[2]:o[3]],
         jnp.zeros((d, LANES - SSM_HEADS), F32)], axis=1).astype(BF16)
    z, xbc, ql, kvl, kpe, dt = _inproj(x2, vec(ln_in_g), vec(ln_in_b), w_packed)

    dtT = jnp.transpose(dt[:, :SSM_HEADS].reshape(batch, seq, SSM_HEADS), (0, 2, 1))
    y_ssm = _ssd(z, xbc, dt, dtT, conv_w[0], conv_b[0], dt_bias[0], a_log[0], d_skip[0], ssm_norm_g[0], batch, seq)

    wq = w_q_b[0].reshape(MLA_Q_RANK, MLA_HEADS, MLA_NOPE + MLA_ROPE)
    wq_pe = wq[:, :, MLA_NOPE:]
    wkv = w_kv_b[0].reshape(MLA_KV_RANK, MLA_HEADS, MLA_NOPE + MLA_V)
    flat = lambda w: w.reshape(w.shape[0], -1).astype(BF16)
    cosq, sinq = _rope_tables(seq)
    q, k, v = _mlaproj(ql, kvl, kpe, cosq, sinq, q_norm_g[0], kv_norm_g[0], flat(wq[:, :, :MLA_NOPE]), flat(wq_pe),
                       flat(_swap_halves(wq_pe)), flat(wkv[:, :, :MLA_NOPE]), flat(wkv[:, :, MLA_NOPE:]), batch, seq)
    y_mla = _flash(q, k, v).reshape(T, MLA_WIDTH)

    wo = w_out[0].astype(BF16)
    h2 = _outproj(x2, y_ssm, y_mla, vec(ln_in_g), vec(ln_in_b), wo[:SSM_WIDTH], wo[SSM_WIDTH:],
                  vec(ln_mix_g[0]), vec(ln_mix_b[0]))

    keys = sub_keys[0].reshape(PEER_HEADS * 2, PEER_KEYS, PEER_HALF).astype(BF16)
    idx, gate = _route(h2, w_query[0].astype(BF16), keys)

    wide = 2 * PEER_SEL
    act_lo, act_hi = _peer_call(_peer_u_kernel, "peer_u", idx, (h2,), _pack_table(u_table[0]), (wide, wide))
    w_lo, w_hi = _peer_w(act_lo, act_hi, gate)
    (ffn,) = _peer_call(_peer_v_kernel, "peer_v", idx, (w_lo, w_hi), _pack_table(v_table[0]), (D_MODEL,))
    out = _final(h2, ffn, vec(ln_ffn_g[0]), vec(ln_ffn_b[0]))
    return out.reshape(batch, seq, d)
```

```python
import functools

import numpy as np
import jax
import jax.numpy as jnp
from jax import lax
from jax.experimental import pallas as pl
from jax.experimental.pallas import tpu as pltpu

F32 = jnp.float32
BF16 = jnp.bfloat16

D_MODEL = 1024
SSM_HEADS = 16
SSM_HEAD_DIM = 64
SSM_WIDTH = SSM_HEADS * SSM_HEAD_DIM
SSM_GROUPS = 2
SSM_STATE = 128
SSM_CONV = 4
SSM_CHUNK = 256
SSM_CONV_WIDTH = SSM_WIDTH + 2 * SSM_GROUPS * SSM_STATE
MLA_HEADS = 8
MLA_Q_RANK = 384
MLA_KV_RANK = 256
MLA_NOPE = 128
MLA_ROPE = 64
MLA_V = 128
MLA_WIDTH = MLA_HEADS * MLA_V
ROPE_THETA = 10000.0
PEER_HEADS = 8
PEER_KEYS = 128
PEER_HALF = 128
PEER_TOPK = 16
PEER_SEL = PEER_HEADS * PEER_TOPK
DEPTH = 1
ALPHA = (2.0 * DEPTH) ** 0.25
EPS = 1e-5

LANES = 128
SUBLANES = 8
MXU_DEPTH = 256
VMEM_LIMIT = 48 * 1024 * 1024
VMEM_LIMIT_TABLE = 56 * 1024 * 1024

ROW_TILE = 512
ATTN_BLOCK = 512
ATTN_Q_TILE = 2 * ATTN_BLOCK
ROUTE_TILE = 256
GATHER_TILE = 512
GATHER_GROUP = SUBLANES
TILE_STRIDE = PEER_SEL + 1
WORDS_PER_ROW = D_MODEL // 2 // LANES

_OFF_Z = 0
_OFF_XBC = _OFF_Z + SSM_WIDTH
_OFF_QL = _OFF_XBC + SSM_CONV_WIDTH
_OFF_KVL = _OFF_QL + MLA_Q_RANK
_OFF_KPE = _OFF_KVL + MLA_KV_RANK
_OFF_DT = _OFF_KPE + LANES
_IN_COLS = _OFF_DT + LANES


def _params(n_grid):
    return pltpu.CompilerParams(dimension_semantics=("arbitrary",) * n_grid, vmem_limit_bytes=VMEM_LIMIT)


def _layer_norm(x, g, b):
    mu = jnp.mean(x, axis=-1, keepdims=True)
    xc = x - mu
    var = jnp.mean(xc * xc, axis=-1, keepdims=True)
    return xc * lax.rsqrt(var + EPS) * g + b


def _rms_norm(x, g):
    return x * lax.rsqrt(jnp.mean(x * x, axis=-1, keepdims=True) + EPS) * g


def _sigmoid(x):
    return 1.0 / (1.0 + jnp.exp(-x))


def _softplus(x):
    return jnp.maximum(x, 0.0) + jnp.log1p(jnp.exp(-jnp.abs(x)))


def _bf16_pieces(a):
    p0 = a.astype(BF16)
    r = a - p0.astype(F32)
    p1 = r.astype(BF16)
    return p0, p1, (r - p1.astype(F32)).astype(BF16)


def _dot_f32_by_01(a, sel, dims=(((1,), (0,)), ((), ()))):
    out = None
    for piece in _bf16_pieces(a):
        t = lax.dot_general(piece, sel, dims, preferred_element_type=F32)
        out = t if out is None else out + t
    return out


def _dot_01_by_f32(sel, a):
    out = None
    for piece in _bf16_pieces(a):
        t = jnp.dot(sel, piece, preferred_element_type=F32)
        out = t if out is None else out + t
    return out


def _inproj_kernel(x_ref, g_ref, b_ref, w_ref, z_ref, xbc_ref, ql_ref, kvl_ref, kpe_ref, dt_ref):
    h = _layer_norm(x_ref[...], g_ref[...], b_ref[...]).astype(BF16)

    def seg(lo, hi):
        return jnp.dot(h, w_ref[:, lo:hi], preferred_element_type=F32)

    z_ref[...] = seg(_OFF_Z, _OFF_XBC)
    xbc_ref[...] = seg(_OFF_XBC, _OFF_QL)
    ql_ref[...] = seg(_OFF_QL, _OFF_KVL)
    kvl_ref[...] = seg(_OFF_KVL, _OFF_KPE)
    kpe_ref[...] = seg(_OFF_KPE, _OFF_DT)
    dt_ref[...] = seg(_OFF_DT, _IN_COLS)


def _inproj(x2, ln_g, ln_b, w_packed):
    T = x2.shape[0]
    widths = (SSM_WIDTH, SSM_CONV_WIDTH, MLA_Q_RANK, MLA_KV_RANK, LANES, LANES)
    row = lambda w: pl.BlockSpec((ROW_TILE, w), lambda i: (i, 0))
    full = lambda a: pl.BlockSpec(a.shape, lambda i: (0,) * a.ndim)
    return pl.pallas_call(
        _inproj_kernel,
        grid=(T // ROW_TILE,),
        in_specs=[row(D_MODEL), full(ln_g), full(ln_b), full(w_packed)],
        out_specs=[row(w) for w in widths],
        out_shape=[jax.ShapeDtypeStruct((T, w), F32) for w in widths],
        compiler_params=_params(1),
        name="inproj",
    )(x2, ln_g, ln_b, w_packed)


def _ssd_kernel(z_ref, xbc_ref, dt_ref, dtT_ref, convw_ref, convb_ref, dtb_ref, dtbT_ref, alog_ref, alogT_ref,
                dskip_ref, g_ref, e_ref, y_ref, xpad_ref, st_ref):
    Q = SSM_CHUNK
    HALO = SUBLANES
    c = pl.program_id(1)

    @pl.when(c == 0)
    def _():
        xpad_ref[0:HALO, :] = jnp.zeros((HALO, SSM_CONV_WIDTH), F32)
        st_ref[...] = jnp.zeros(st_ref.shape, F32)

    raw = xbc_ref[...]
    xpad_ref[HALO:HALO + Q, :] = raw
    conv = convb_ref[...] + raw * convw_ref[SSM_CONV - 1:SSM_CONV, :]
    for k in range(SSM_CONV - 1):
        off = HALO - (SSM_CONV - 1) + k
        conv = conv + xpad_ref[off:off + Q, :] * convw_ref[k:k + 1, :]
    xpad_ref[0:HALO, :] = raw[Q - HALO:Q, :]
    xbc = conv * _sigmoid(conv)
    xs = xbc[:, :SSM_WIDTH]
    bm = xbc[:, SSM_WIDTH:SSM_WIDTH + SSM_GROUPS * SSM_STATE]
    cm = xbc[:, SSM_WIDTH + SSM_GROUPS * SSM_STATE:]

    dt = _softplus(dt_ref[...] + dtb_ref[...])
    dtT = _softplus(dtT_ref[...] + dtbT_ref[...])
    a = dt * (-jnp.exp(alog_ref[...]))
    aT = dtT * (-jnp.exp(alogT_ref[...]))
    row = lax.broadcasted_iota(jnp.int32, (Q, Q), 0)
    col = lax.broadcasted_iota(jnp.int32, (Q, Q), 1)
    causal = col <= row
    lower = jnp.where(causal, 1.0, 0.0).astype(BF16)
    upper = jnp.where(row <= col, 1.0, 0.0).astype(BF16)
    a_cum = _dot_01_by_f32(lower, a)
    a_cumT = _dot_f32_by_01(aT, upper)
    a_last = a_cum[Q - 1:Q, :]

    expand = e_ref[...]
    ex = lambda v: _dot_f32_by_01(v, expand)
    xdt = xs * ex(dt)
    decay_out = ex(jnp.exp(a_cum))
    decay_st = ex(jnp.exp(a_last - a_cum))
    chunk_decay = ex(jnp.broadcast_to(jnp.exp(a_last), (SUBLANES, LANES)))[0:1, :]
    xdt_b = xdt.astype(BF16)
    wx_b = (decay_st * xdt).astype(BF16)

    heads_per_group = SSM_HEADS // SSM_GROUPS
    gw = heads_per_group * SSM_HEAD_DIM
    y_parts = []
    for g in range(SSM_GROUPS):
        bg = bm[:, g * SSM_STATE:(g + 1) * SSM_STATE]
        cg = cm[:, g * SSM_STATE:(g + 1) * SSM_STATE].astype(BF16)
        cb = lax.dot_general(cg, bg.astype(BF16), (((1,), (1,)), ((), ())), preferred_element_type=F32)
        yd = []
        for e in range(heads_per_group):
            h = g * heads_per_group + e
            seg = a_cum[:, h:h + 1] - a_cumT[h:h + 1, :]
            m = (cb * jnp.exp(jnp.where(causal, seg, -jnp.inf))).astype(BF16)
            yd.append(jnp.dot(m, xdt_b[:, h * SSM_HEAD_DIM:(h + 1) * SSM_HEAD_DIM], preferred_element_type=F32))
        yd = jnp.concatenate(yd, axis=1)
        st = st_ref[g]
        y_off = jnp.dot(cg, st.astype(BF16), preferred_element_type=F32) * decay_out[:, g * gw:(g + 1) * gw]
        y_parts.append(yd + y_off)
        bgT = jnp.transpose(bg).astype(BF16)
        st_ref[g] = st * chunk_decay[:, g * gw:(g + 1) * gw] + jnp.dot(
            bgT, wx_b[:, g * gw:(g + 1) * gw], preferred_element_type=F32)
    y = jnp.concatenate(y_parts, axis=1)
    d_e = ex(jnp.broadcast_to(dskip_ref[...], (SUBLANES, LANES)))[0:1, :]
    y = y + d_e * xs
    zz = z_ref[...]
    y = y * (zz * _sigmoid(zz))
    y_ref[...] = _rms_norm(y, g_ref[...]).astype(BF16)


def _ssd(z, xbc, dt, dtT, conv_w, conv_b, dt_bias, a_log, d_skip, norm_g, batch, seq):
    T = z.shape[0]
    nc = seq // SSM_CHUNK
    pad = lambda v: jnp.pad(v.reshape(1, -1), ((0, 0), (0, LANES - v.shape[-1])))
    colv = lambda v: v.reshape(-1, 1)
    head_of_lane = np.arange(SSM_WIDTH) // SSM_HEAD_DIM
    expand = jnp.asarray((np.arange(LANES)[:, None] == head_of_lane[None, :]).astype(np.float32)).astype(BF16)
    row = lambda w: pl.BlockSpec((SSM_CHUNK, w), lambda b, c: (b * nc + c, 0))
    full = lambda a: pl.BlockSpec(a.shape, lambda b, c: (0,) * a.ndim)
    args = (z, xbc, dt, dtT, conv_w, conv_b.reshape(1, -1), pad(dt_bias), colv(dt_bias), pad(a_log), colv(a_log),
            pad(d_skip), norm_g.reshape(1, -1), expand)
    in_specs = [row(SSM_WIDTH), row(SSM_CONV_WIDTH), row(LANES),
                pl.BlockSpec((None, SSM_HEADS, SSM_CHUNK), lambda b, c: (b, 0, c))]
    in_specs += [full(a) for a in args[4:]]
    return pl.pallas_call(
        _ssd_kernel,
        grid=(batch, nc),
        in_specs=in_specs,
        out_specs=row(SSM_WIDTH),
        out_shape=jax.ShapeDtypeStruct((T, SSM_WIDTH), BF16),
        scratch_shapes=[pltpu.VMEM((SUBLANES + SSM_CHUNK, SSM_CONV_WIDTH), F32),
                        pltpu.VMEM((SSM_GROUPS, SSM_STATE, SSM_WIDTH // SSM_GROUPS), F32)],
        compiler_params=_params(2),
        name="ssd",
    )(*args)


def _mlaproj_kernel(ql_ref, kvl_ref, kpe_ref, cos_ref, sin_ref, qg_ref, kvg_ref, wqn_ref, wqp_ref, wqs_ref,
                    wkn_ref, wv_ref, q_ref, k_ref, v_ref):
    scale = (MLA_NOPE + MLA_ROPE) ** -0.5
    cos = cos_ref[...]
    sin = sin_ref[...]
    qn = _rms_norm(ql_ref[...], qg_ref[...]).astype(BF16)
    q_nope = jnp.dot(qn, wqn_ref[...], preferred_element_type=F32) * scale
    q_pe = (jnp.dot(qn, wqp_ref[...], preferred_element_type=F32) * cos
            + jnp.dot(qn, wqs_ref[...], preferred_element_type=F32) * sin) * scale
    kvn = _rms_norm(kvl_ref[...], kvg_ref[...]).astype(BF16)
    k_nope = jnp.dot(kvn, wkn_ref[...], preferred_element_type=F32)
    v = jnp.dot(kvn, wv_ref[...], preferred_element_type=F32)
    kpe = kpe_ref[...]
    k_pe = kpe[:, :MLA_ROPE] * cos[:, :MLA_ROPE] + kpe[:, MLA_ROPE:] * sin[:, :MLA_ROPE]
    zpad = jnp.zeros((q_nope.shape[0], MXU_DEPTH - MLA_NOPE - MLA_ROPE), F32)
    for h in range(MLA_HEADS):
        q_ref[h] = jnp.concatenate(
            [q_nope[:, h * MLA_NOPE:(h + 1) * MLA_NOPE], q_pe[:, h * MLA_ROPE:(h + 1) * MLA_ROPE], zpad],
            axis=1).astype(BF16)
        k_ref[h] = jnp.concatenate([k_nope[:, h * MLA_NOPE:(h + 1) * MLA_NOPE], k_pe, zpad], axis=1).astype(BF16)
        v_ref[h] = v[:, h * MLA_V:(h + 1) * MLA_V].astype(BF16)


def _mlaproj(ql, kvl, kpe, cosq, sinq, q_g, kv_g, wqn, wqp, wqs, wkn, wv, batch, seq):
    T = ql.shape[0]
    per_seq = seq // ROW_TILE
    row = lambda w: pl.BlockSpec((ROW_TILE, w), lambda i: (i, 0))
    pos = lambda w: pl.BlockSpec((ROW_TILE, w), lambda i: (i % per_seq, 0))
    full = lambda a: pl.BlockSpec(a.shape, lambda i: (0,) * a.ndim)
    head = lambda w: pl.BlockSpec((None, MLA_HEADS, ROW_TILE, w), lambda i: (i // per_seq, 0, i % per_seq, 0))
    consts = (q_g.reshape(1, -1), kv_g.reshape(1, -1), wqn, wqp, wqs, wkn, wv)
    return pl.pallas_call(
        _mlaproj_kernel,
        grid=(T // ROW_TILE,),
        in_specs=[row(MLA_Q_RANK), row(MLA_KV_RANK), row(LANES), pos(MLA_HEADS * MLA_ROPE), pos(MLA_HEADS * MLA_ROPE)]
        + [full(a) for a in consts],
        out_specs=[head(MXU_DEPTH), head(MXU_DEPTH), head(MLA_V)],
        out_shape=[jax.ShapeDtypeStruct((batch, MLA_HEADS, seq, MXU_DEPTH), BF16),
                   jax.ShapeDtypeStruct((batch, MLA_HEADS, seq, MXU_DEPTH), BF16),
                   jax.ShapeDtypeStruct((batch, MLA_HEADS, seq, MLA_V), BF16)],
        compiler_params=_params(1),
        name="mlaproj",
    )(ql, kvl, kpe, cosq, sinq, *consts)


def _flash_kernel(q_ref, k_ref, v_ref, o_ref, m_ref, l_ref, acc_ref):
    blk = ATTN_BLOCK
    halves = ATTN_Q_TILE // blk
    qi = pl.program_id(2)
    m_ref[...] = jnp.full(m_ref.shape, -jnp.inf, F32)
    l_ref[...] = jnp.zeros(l_ref.shape, F32)
    acc_ref[...] = jnp.zeros(acc_ref.shape, F32)
    below_diag = (lax.broadcasted_iota(jnp.int32, (blk, blk), 1) <= lax.broadcasted_iota(jnp.int32, (blk, blk), 0))

    def lane_tiles(x):
        return [x[:, j * LANES:(j + 1) * LANES] for j in range(x.shape[1] // LANES)]

    def step(half, kv_block, masked):
        rows = pl.ds(half * blk, blk)
        off = pl.multiple_of(kv_block * blk, blk)
        s = lax.dot_general(q_ref[rows, :], k_ref[pl.ds(off, blk), :], (((1,), (1,)), ((), ())),
                            preferred_element_type=F32)
        if masked:
            s = jnp.where(below_diag, s, -jnp.inf)
        m_prev = m_ref[rows, :]
        m_tile = functools.reduce(jnp.maximum, lane_tiles(s))
        m_new = jnp.maximum(m_prev, jnp.broadcast_to(jnp.max(m_tile, axis=-1, keepdims=True), (blk, LANES)))
        p = jnp.exp(s - jnp.tile(m_new, (1, blk // LANES)))
        alpha = jnp.exp(m_prev - m_new)
        p_tile = functools.reduce(lambda a, b: a + b, lane_tiles(p))
        l_ref[rows, :] = alpha * l_ref[rows, :] + jnp.broadcast_to(jnp.sum(p_tile, axis=-1, keepdims=True),
                                                                   (blk, LANES))
        acc_ref[rows, :] = alpha * acc_ref[rows, :] + jnp.dot(p.astype(BF16), v_ref[pl.ds(off, blk), :],
                                                              preferred_element_type=F32)
        m_ref[rows, :] = m_new

    def bulk(j, carry):
        for half in range(halves):
            step(half, j, False)
        return carry

    first = qi * halves
    lax.fori_loop(0, first, bulk, 0)
    for half in range(halves):
        for d in range(half):
            step(half, first + d, False)
        step(half, first + half, True)
    o_ref[...] = (acc_ref[...] / l_ref[...]).astype(BF16)


def _flash(q, k, v):
    batch, heads, seq, _ = q.shape
    qspec = pl.BlockSpec((None, None, ATTN_Q_TILE, MXU_DEPTH), lambda b, h, i: (b, h, i, 0))
    kspec = pl.BlockSpec((None, None, seq, MXU_DEPTH), lambda b, h, i: (b, h, 0, 0))
    vspec = pl.BlockSpec((None, None, seq, MLA_V), lambda b, h, i: (b, h, 0, 0))
    return pl.pallas_call(
        _flash_kernel,
        grid=(batch, heads, seq // ATTN_Q_TILE),
        in_specs=[qspec, kspec, vspec],
        out_specs=pl.BlockSpec((None, ATTN_Q_TILE, MLA_V), lambda b, h, i: (b, i, h)),
        out_shape=jax.ShapeDtypeStruct((batch, seq, heads * MLA_V), BF16),
        scratch_shapes=[pltpu.VMEM((ATTN_Q_TILE, LANES), F32), pltpu.VMEM((ATTN_Q_TILE, LANES), F32),
                        pltpu.VMEM((ATTN_Q_TILE, MLA_V), F32)],
        compiler_params=_params(3),
        name="flash",
    )(q, k, v)


def _outproj_kernel(x_ref, ys_ref, ym_ref, g0_ref, b0_ref, wa_ref, wb_ref, g1_ref, b1_ref, o_ref):
    h = _layer_norm(x_ref[...], g0_ref[...], b0_ref[...])
    mix = (jnp.dot(ys_ref[...], wa_ref[...], preferred_element_type=F32)
           + jnp.dot(ym_ref[...], wb_ref[...], preferred_element_type=F32))
    o_ref[...] = _layer_norm(ALPHA * h + mix, g1_ref[...], b1_ref[...])


def _outproj(x2, y_ssm, y_mla, g0, b0, wa, wb, g1, b1):
    T = x2.shape[0]
    row = lambda w: pl.BlockSpec((ROW_TILE, w), lambda i: (i, 0))
    full = lambda a: pl.BlockSpec(a.shape, lambda i: (0,) * a.ndim)
    consts = (g0, b0, wa, wb, g1, b1)
    return pl.pallas_call(
        _outproj_kernel,
        grid=(T // ROW_TILE,),
        in_specs=[row(D_MODEL), row(SSM_WIDTH), row(MLA_WIDTH)] + [full(a) for a in consts],
        out_specs=row(D_MODEL),
        out_shape=jax.ShapeDtypeStruct((T, D_MODEL), F32),
        compiler_params=_params(1),
        name="outproj",
    )(x2, y_ssm, y_mla, *consts)


def _top_rows(s, k, payload=None):
    n = s.shape[0]
    rows = lax.broadcasted_iota(jnp.int32, s.shape, 0).astype(F32)
    vals, ids = [], []
    for _ in range(k):
        m = jnp.max(s, axis=0, keepdims=True)
        first = jnp.min(jnp.where(s == m, rows, float(n)), axis=0, keepdims=True)
        hit = rows == first
        vals.append(m)
        ids.append(first if payload is None else jnp.sum(jnp.where(hit, payload, 0.0), axis=0, keepdims=True))
        s = jnp.where(hit, -jnp.inf, s)
    return vals, ids


def _pair_candidates(v1, v2, combine):
    K = PEER_TOPK
    v1s = jnp.concatenate(v1, axis=0)
    v2s = jnp.concatenate(v2, axis=0)
    pieces = [combine(v1[0], v2s)]
    pieces += [combine(v1[a], v2s[:SUBLANES]) for a in range(1, SUBLANES)]
    pieces.append(combine(v1s[SUBLANES:], v2[0]))
    return jnp.concatenate(pieces, axis=0)


def _route_kernel(h_ref, wq_ref, keys_ref, idx_ref, gate_ref, off_scr, gate_scr):
    hb = h_ref[...].astype(BF16)
    K = PEER_TOPK
    for h in range(PEER_HEADS):
        tops = []
        for half in range(2):
            c0 = (h * 2 + half) * PEER_HALF
            q = jnp.dot(hb, wq_ref[:, c0:c0 + PEER_HALF], preferred_element_type=F32)
            sT = lax.dot_general(keys_ref[h * 2 + half], q.astype(BF16), (((1,), (1,)), ((), ())),
                                 preferred_element_type=F32)
            tops.append(_top_rows(sT, K))
        (v1, i1), (v2, i2) = tops
        cand_s = _pair_candidates(v1, v2, lambda x, y: x + y)
        cand_e = _pair_candidates(i1, i2, lambda x, y: x * float(PEER_KEYS) + y)
        top_s, top_e = _top_rows(cand_s, K, payload=cand_e)
        ex = [jnp.exp(s - top_s[0]) for s in top_s]
        denom = ex[0]
        for e in ex[1:]:
            denom = denom + e
        inv = 1.0 / denom
        for j in range(K):
            r = h * K + j
            off_scr[r:r + 1, :] = top_e[j] * float(WORDS_PER_ROW)
            gate_scr[r:r + 1, :] = ex[j] * inv
    idx_ref[...] = jnp.transpose(off_scr[...]).astype(jnp.int32)
    gate_ref[...] = jnp.transpose(gate_scr[...])


def _route(h2, wq, keys):
    T = h2.shape[0]
    full = lambda a: pl.BlockSpec(a.shape, lambda i: (0,) * a.ndim)
    out = pl.BlockSpec((ROUTE_TILE, PEER_SEL), lambda i: (i, 0))
    return pl.pallas_call(
        _route_kernel,
        grid=(T // ROUTE_TILE,),
        in_specs=[pl.BlockSpec((ROUTE_TILE, D_MODEL), lambda i: (i, 0)), full(wq), full(keys)],
        out_specs=[out, out],
        out_shape=[jax.ShapeDtypeStruct((T, PEER_SEL), jnp.int32), jax.ShapeDtypeStruct((T, PEER_SEL), F32)],
        scratch_shapes=[pltpu.VMEM((PEER_SEL, ROUTE_TILE), F32), pltpu.VMEM((PEER_SEL, ROUTE_TILE), F32)],
        compiler_params=_params(1),
        name="route",
    )(h2, wq, keys)


def _gather_group(idx_ref, tbl_ref, tile_ref, g):
    S = TILE_STRIDE
    for q in range(GATHER_GROUP):
        for kk in range(PEER_SEL):
            i = pl.multiple_of(idx_ref[g * GATHER_GROUP + q, kk], WORDS_PER_ROW)
            tile_ref[pl.ds(q * WORDS_PER_ROW * S + kk, WORDS_PER_ROW, stride=S), :] = tbl_ref[pl.ds(i, WORDS_PER_ROW), :]


def _tile_chunk(tile_ref, q, j):
    S = TILE_STRIDE
    return pltpu.bitcast(tile_ref[pl.ds(q * WORDS_PER_ROW * S + j * S, PEER_SEL), :], BF16)


def _pipelined_groups(idx_ref, tbl_ref, tile_a, tile_b, contract):
    groups = GATHER_TILE // GATHER_GROUP
    _gather_group(idx_ref, tbl_ref, tile_a, 0)

    def pair(p, carry):
        g = 2 * p
        _gather_group(idx_ref, tbl_ref, tile_b, g + 1)
        contract(tile_a, g)
        _gather_group(idx_ref, tbl_ref, tile_a, jnp.minimum(g + 2, groups - 1))
        contract(tile_b, g + 1)
        return carry

    lax.fori_loop(0, groups // 2, pair, 0)


def _keep_own_rows(acc, q, lo, hi):
    own = lax.broadcasted_iota(jnp.int32, lo.shape, 0) == q
    return jnp.where(own, acc[:GATHER_GROUP], lo), jnp.where(own, acc[GATHER_GROUP:], hi)


def _peer_u_kernel(idx_ref, x_ref, tbl_ref, lo_ref, hi_ref, tile_a, tile_b):
    half = D_MODEL // 2

    def contract(tile_ref, g):
        r0 = pl.multiple_of(GATHER_GROUP * g, GATHER_GROUP)
        x = x_ref[pl.ds(r0, GATHER_GROUP), :]
        lhs = jnp.concatenate([x[:, :half], x[:, half:]], axis=0).astype(BF16)
        lo = jnp.zeros((GATHER_GROUP, 2 * PEER_SEL), F32)
        hi = jnp.zeros((GATHER_GROUP, 2 * PEER_SEL), F32)
        for q in range(GATHER_GROUP):
            acc = jnp.zeros((2 * GATHER_GROUP, 2 * PEER_SEL), F32)
            for j in range(WORDS_PER_ROW):
                acc = acc + lax.dot_general(lhs[:, j * LANES:(j + 1) * LANES], _tile_chunk(tile_ref, q, j),
                                            (((1,), (1,)), ((), ())), preferred_element_type=F32)
            lo, hi = _keep_own_rows(acc, q, lo, hi)
        lo_ref[pl.ds(r0, GATHER_GROUP), :] = lo
        hi_ref[pl.ds(r0, GATHER_GROUP), :] = hi

    _pipelined_groups(idx_ref, tbl_ref, tile_a, tile_b, contract)


def _peer_v_kernel(idx_ref, wlo_ref, whi_ref, tbl_ref, o_ref, tile_a, tile_b):
    half = D_MODEL // 2

    def contract(tile_ref, g):
        r0 = pl.multiple_of(GATHER_GROUP * g, GATHER_GROUP)
        rows = pl.ds(r0, GATHER_GROUP)
        lhs = jnp.concatenate([wlo_ref[rows, :], whi_ref[rows, :]], axis=0).astype(BF16)
        for j in range(WORDS_PER_ROW):
            lo = jnp.zeros((GATHER_GROUP, LANES), F32)
            hi = jnp.zeros((GATHER_GROUP, LANES), F32)
            for q in range(GATHER_GROUP):
                r = jnp.dot(lhs, _tile_chunk(tile_ref, q, j), preferred_element_type=F32)
                lo, hi = _keep_own_rows(r, q, lo, hi)
            o_ref[rows, j * LANES:(j + 1) * LANES] = lo
            o_ref[rows, half + j * LANES:half + (j + 1) * LANES] = hi

    _pipelined_groups(idx_ref, tbl_ref, tile_a, tile_b, contract)


def _peer_call(kernel, name, idx, row_inputs, table, out_widths):
    T = idx.shape[0]
    tile = pltpu.VMEM((GATHER_GROUP * WORDS_PER_ROW * TILE_STRIDE, LANES), jnp.int32)
    row = lambda w: pl.BlockSpec((GATHER_TILE, w), lambda i: (i, 0))
    return pl.pallas_call(
        kernel,
        grid=(T // GATHER_TILE,),
        in_specs=[pl.BlockSpec((GATHER_TILE, PEER_SEL), lambda i: (i, 0), memory_space=pltpu.SMEM)]
        + [row(a.shape[1]) for a in row_inputs] + [pl.BlockSpec(memory_space=pltpu.VMEM)],
        out_specs=[row(w) for w in out_widths],
        out_shape=[jax.ShapeDtypeStruct((T, w), F32) for w in out_widths],
        scratch_shapes=[tile, tile],
        compiler_params=pltpu.CompilerParams(dimension_semantics=("arbitrary",), vmem_limit_bytes=VMEM_LIMIT_TABLE),
        name=name,
    )(idx, *row_inputs, table)


def _peer_w_kernel(lo_ref, hi_ref, gate_ref, sel_e_ref, sel_o_ref, wlo_ref, whi_ref):
    pick = (((1,), (1,)), ((), ()))
    act = _dot_f32_by_01(lo_ref[...], sel_e_ref[...], pick) + _dot_f32_by_01(hi_ref[...], sel_o_ref[...], pick)
    w = gate_ref[...] * (0.5 * act * (1.0 + lax.erf(act * (2.0 ** -0.5))))
    wlo_ref[...] = _dot_f32_by_01(w, sel_e_ref[...])
    whi_ref[...] = _dot_f32_by_01(w, sel_o_ref[...])


def _peer_w(act_lo, act_hi, gate):
    T = gate.shape[0]
    k = np.arange(PEER_SEL)
    sel_e = np.zeros((PEER_SEL, 2 * PEER_SEL), np.float32)
    sel_e[k, 2 * k] = 1.0
    sel_o = np.zeros((PEER_SEL, 2 * PEER_SEL), np.float32)
    sel_o[k, 2 * k + 1] = 1.0
    sel_e, sel_o = jnp.asarray(sel_e).astype(BF16), jnp.asarray(sel_o).astype(BF16)
    row = lambda w: pl.BlockSpec((ROW_TILE, w), lambda i: (i, 0))
    full = lambda a: pl.BlockSpec(a.shape, lambda i: (0,) * a.ndim)
    wide = 2 * PEER_SEL
    return pl.pallas_call(
        _peer_w_kernel,
        grid=(T // ROW_TILE,),
        in_specs=[row(wide), row(wide), row(PEER_SEL), full(sel_e), full(sel_o)],
        out_specs=[row(wide), row(wide)],
        out_shape=[jax.ShapeDtypeStruct((T, wide), F32), jax.ShapeDtypeStruct((T, wide), F32)],
        compiler_params=_params(1),
        name="peer_w",
    )(act_lo, act_hi, gate, sel_e, sel_o)


def _final_kernel(h_ref, f_ref, g_ref, b_ref, o_ref):
    o_ref[...] = _layer_norm(ALPHA * h_ref[...] + f_ref[...], g_ref[...], b_ref[...])


def _final(h2, ffn, g, b):
    T = h2.shape[0]
    row = pl.BlockSpec((ROW_TILE, D_MODEL), lambda i: (i, 0))
    full = lambda a: pl.BlockSpec(a.shape, lambda i: (0,) * a.ndim)
    return pl.pallas_call(
        _final_kernel,
        grid=(T // ROW_TILE,),
        in_specs=[row, row, full(g), full(b)],
        out_specs=row,
        out_shape=jax.ShapeDtypeStruct((T, D_MODEL), F32),
        compiler_params=_params(1),
        name="final",
    )(h2, ffn, g, b)


def _pack_table(tab):
    bits = lax.bitcast_convert_type(tab.astype(BF16), jnp.uint16).astype(jnp.uint32)
    half = D_MODEL // 2
    words = bits[:, :half] | (bits[:, half:] << 16)
    return lax.bitcast_convert_type(words, jnp.int32).reshape(-1, LANES)


def _rope_tables(seq):
    half = MLA_ROPE // 2
    inv_freq = ROPE_THETA ** (-jnp.arange(half, dtype=F32) / half)
    ang = jnp.arange(seq, dtype=F32)[:, None] * inv_freq
    cos, sin = jnp.cos(ang), jnp.sin(ang)
    cos64 = jnp.concatenate([cos, cos], axis=1)
    sin64 = jnp.concatenate([-sin, sin], axis=1)
    return jnp.tile(cos64, (1, MLA_HEADS)), jnp.tile(sin64, (1, MLA_HEADS))


def _swap_halves(w):
    half = w.shape[-1] // 2
    return jnp.concatenate([w[..., half:], w[..., :half]], axis=-1)


def kernel(x, ln_in_g, ln_in_b, w_in, conv_w, conv_b, dt_bias, a_log, d_skip, ssm_norm_g, q_norm_g, w_q_b, kv_norm_g, w_kv_b, w_out, ln_mix_g, ln_mix_b, w_query, sub_keys, u_table, v_table, ln_ffn_g, ln_ffn_b):
    batch, seq, d = x.shape
    assert d == D_MODEL and w_in.shape[0] == DEPTH == 1
    assert seq % ROW_TILE == 0 and seq % SSM_CHUNK == 0 and seq % ATTN_Q_TILE == 0
    T = batch * seq
    assert T % GATHER_TILE == 0 and T % ROUTE_TILE == 0
    x2 = x.reshape(T, d)
    vec = lambda v: v.reshape(1, -1)

    wi = w_in[0]
    o = np.cumsum((0, SSM_WIDTH, SSM_CONV_WIDTH, SSM_HEADS, MLA_Q_RANK, MLA_KV_RANK, MLA_ROPE))
    w_kpe = wi[:, o[5]:o[6]]
    w_packed = jnp.concatenate(
        [wi[:, o[0]:o[2]], wi[:, o[3]:o[5]], w_kpe, _swap_halves(w_kpe), wi[:, o[2]:o[3]],
         jnp.zeros((d, LANES - SSM_HEADS), F32)], axis=1).astype(BF16)
    z, xbc, ql, kvl, kpe, dt = _inproj(x2, vec(ln_in_g), vec(ln_in_b), w_packed)

    dtT = jnp.transpose(dt[:, :SSM_HEADS].reshape(batch, seq, SSM_HEADS), (0, 2, 1))
    y_ssm = _ssd(z, xbc, dt, dtT, conv_w[0], conv_b[0], dt_bias[0], a_log[0], d_skip[0], ssm_norm_g[0], batch, seq)

    wq = w_q_b[0].reshape(MLA_Q_RANK, MLA_HEADS, MLA_NOPE + MLA_ROPE)
    wq_pe = wq[:, :, MLA_NOPE:]
    wkv = w_kv_b[0].reshape(MLA_KV_RANK, MLA_HEADS, MLA_NOPE + MLA_V)
    flat = lambda w: w.reshape(w.shape[0], -1).astype(BF16)
    cosq, sinq = _rope_tables(seq)
    q, k, v = _mlaproj(ql, kvl, kpe, cosq, sinq, q_norm_g[0], kv_norm_g[0], flat(wq[:, :, :MLA_NOPE]), flat(wq_pe),
                       flat(_swap_halves(wq_pe)), flat(wkv[:, :, :MLA_NOPE]), flat(wkv[:, :, MLA_NOPE:]), batch, seq)
    y_mla = _flash(q, k, v).reshape(T, MLA_WIDTH)

    wo = w_out[0].astype(BF16)
    h2 = _outproj(x2, y_ssm, y_mla, vec(ln_in_g), vec(ln_in_b), wo[:SSM_WIDTH], wo[SSM_WIDTH:],
                  vec(ln_mix_g[0]), vec(ln_mix_b[0]))

    keys = sub_keys[0].reshape(PEER_HEADS * 2, PEER_KEYS, PEER_HALF).astype(BF16)
    idx, gate = _route(h2, w_query[0].astype(BF16), keys)

    wide = 2 * PEER_SEL
    act_lo, act_hi = _peer_call(_peer_u_kernel, "peer_u", idx, (h2,), _pack_table(u_table[0]), (wide, wide))
    w_lo, w_hi = _peer_w(act_lo, act_hi, gate)
    (ffn,) = _peer_call(_peer_v_kernel, "peer_v", idx, (w_lo, w_hi), _pack_table(v_table[0]), (D_MODEL,))
    out = _final(h2, ffn, vec(ln_ffn_g[0]), vec(ln_ffn_b[0]))
    return out.reshape(batch, seq, d)
```

```python
import functools

import numpy as np
import jax
import jax.numpy as jnp
from jax import lax
from jax.experimental import pallas as pl
from jax.experimental.pallas import tpu as pltpu

F32 = jnp.float32
BF16 = jnp.bfloat16

D_MODEL = 1024
SSM_HEADS = 16
SSM_HEAD_DIM = 64
SSM_WIDTH = SSM_HEADS * SSM_HEAD_DIM
SSM_GROUPS = 2
SSM_STATE = 128
SSM_CONV = 4
SSM_CHUNK = 256
SSM_CONV_WIDTH = SSM_WIDTH + 2 * SSM_GROUPS * SSM_STATE
MLA_HEADS = 8
MLA_Q_RANK = 384
MLA_KV_RANK = 256
MLA_NOPE = 128
MLA_ROPE = 64
MLA_V = 128
MLA_WIDTH = MLA_HEADS * MLA_V
ROPE_THETA = 10000.0
PEER_HEADS = 8
PEER_KEYS = 128
PEER_HALF = 128
PEER_TOPK = 16
PEER_SEL = PEER_HEADS * PEER_TOPK
DEPTH = 1
ALPHA = (2.0 * DEPTH) ** 0.25
EPS = 1e-5

LANES = 128
SUBLANES = 8
MXU_DEPTH = 256
VMEM_LIMIT = 48 * 1024 * 1024
VMEM_LIMIT_TABLE = 56 * 1024 * 1024

ROW_TILE = 512
ATTN_BLOCK = 1024
ATTN_Q_TILE = 2 * ATTN_BLOCK
ROUTE_TILE = 256
GATHER_TILE = 512
GATHER_GROUP = SUBLANES
TILE_STRIDE = PEER_SEL + 1
WORDS_PER_ROW = D_MODEL // 2 // LANES

_OFF_Z = 0
_OFF_XBC = _OFF_Z + SSM_WIDTH
_OFF_QL = _OFF_XBC + SSM_CONV_WIDTH
_OFF_KVL = _OFF_QL + MLA_Q_RANK
_OFF_KPE = _OFF_KVL + MLA_KV_RANK
_OFF_DT = _OFF_KPE + LANES
_IN_COLS = _OFF_DT + LANES


def _params(n_grid):
    return pltpu.CompilerParams(dimension_semantics=("arbitrary",) * n_grid, vmem_limit_bytes=VMEM_LIMIT)


def _layer_norm(x, g, b):
    mu = jnp.mean(x, axis=-1, keepdims=True)
    xc = x - mu
    var = jnp.mean(xc * xc, axis=-1, keepdims=True)
    return xc * lax.rsqrt(var + EPS) * g + b


def _rms_norm(x, g):
    return x * lax.rsqrt(jnp.mean(x * x, axis=-1, keepdims=True) + EPS) * g


def _sigmoid(x):
    return 1.0 / (1.0 + jnp.exp(-x))


def _softplus(x):
    return jnp.maximum(x, 0.0) + jnp.log1p(jnp.exp(-jnp.abs(x)))


def _bf16_pieces(a):
    p0 = a.astype(BF16)
    r = a - p0.astype(F32)
    p1 = r.astype(BF16)
    return p0, p1, (r - p1.astype(F32)).astype(BF16)


def _dot_f32_by_01(a, sel, dims=(((1,), (0,)), ((), ()))):
    out = None
    for piece in _bf16_pieces(a):
        t = lax.dot_general(piece, sel, dims, preferred_element_type=F32)
        out = t if out is None else out + t
    return out


def _dot_01_by_f32(sel, a):
    out = None
    for piece in _bf16_pieces(a):
        t = jnp.dot(sel, piece, preferred_element_type=F32)
        out = t if out is None else out + t
    return out


def _inproj_kernel(x_ref, g_ref, b_ref, w_ref, z_ref, xbc_ref, ql_ref, kvl_ref, kpe_ref, dt_ref):
    h = _layer_norm(x_ref[...], g_ref[...], b_ref[...]).astype(BF16)

    def seg(lo, hi):
        return jnp.dot(h, w_ref[:, lo:hi], preferred_element_type=F32)

    z_ref[...] = seg(_OFF_Z, _OFF_XBC)
    xbc_ref[...] = seg(_OFF_XBC, _OFF_QL)
    ql_ref[...] = seg(_OFF_QL, _OFF_KVL)
    kvl_ref[...] = seg(_OFF_KVL, _OFF_KPE)
    kpe_ref[...] = seg(_OFF_KPE, _OFF_DT)
    dt_ref[...] = seg(_OFF_DT, _IN_COLS)


def _inproj(x2, ln_g, ln_b, w_packed):
    T = x2.shape[0]
    widths = (SSM_WIDTH, SSM_CONV_WIDTH, MLA_Q_RANK, MLA_KV_RANK, LANES, LANES)
    row = lambda w: pl.BlockSpec((ROW_TILE, w), lambda i: (i, 0))
    full = lambda a: pl.BlockSpec(a.shape, lambda i: (0,) * a.ndim)
    return pl.pallas_call(
        _inproj_kernel,
        grid=(T // ROW_TILE,),
        in_specs=[row(D_MODEL), full(ln_g), full(ln_b), full(w_packed)],
        out_specs=[row(w) for w in widths],
        out_shape=[jax.ShapeDtypeStruct((T, w), F32) for w in widths],
        compiler_params=_params(1),
        name="inproj",
    )(x2, ln_g, ln_b, w_packed)


def _ssd_kernel(z_ref, xbc_ref, dt_ref, dtT_ref, convw_ref, convb_ref, dtb_ref, dtbT_ref, alog_ref, alogT_ref,
                dskip_ref, g_ref, e_ref, y_ref, xpad_ref, st_ref):
    Q = SSM_CHUNK
    HALO = SUBLANES
    c = pl.program_id(1)

    @pl.when(c == 0)
    def _():
        xpad_ref[0:HALO, :] = jnp.zeros((HALO, SSM_CONV_WIDTH), F32)
        st_ref[...] = jnp.zeros(st_ref.shape, F32)

    raw = xbc_ref[...]
    xpad_ref[HALO:HALO + Q, :] = raw
    conv = convb_ref[...] + raw * convw_ref[SSM_CONV - 1:SSM_CONV, :]
    for k in range(SSM_CONV - 1):
        off = HALO - (SSM_CONV - 1) + k
        conv = conv + xpad_ref[off:off + Q, :] * convw_ref[k:k + 1, :]
    xpad_ref[0:HALO, :] = raw[Q - HALO:Q, :]
    xbc = conv * _sigmoid(conv)
    xs = xbc[:, :SSM_WIDTH]
    bm = xbc[:, SSM_WIDTH:SSM_WIDTH + SSM_GROUPS * SSM_STATE]
    cm = xbc[:, SSM_WIDTH + SSM_GROUPS * SSM_STATE:]

    dt = _softplus(dt_ref[...] + dtb_ref[...])
    dtT = _softplus(dtT_ref[...] + dtbT_ref[...])
    a = dt * (-jnp.exp(alog_ref[...]))
    aT = dtT * (-jnp.exp(alogT_ref[...]))
    row = lax.broadcasted_iota(jnp.int32, (Q, Q), 0)
    col = lax.broadcasted_iota(jnp.int32, (Q, Q), 1)
    causal = col <= row
    lower = jnp.where(causal, 1.0, 0.0).astype(BF16)
    upper = jnp.where(row <= col, 1.0, 0.0).astype(BF16)
    a_cum = _dot_01_by_f32(lower, a)
    a_cumT = _dot_f32_by_01(aT, upper)
    a_last = a_cum[Q - 1:Q, :]

    expand = e_ref[...]
    ex = lambda v: _dot_f32_by_01(v, expand)
    xdt = xs * ex(dt)
    decay_out = ex(jnp.exp(a_cum))
    decay_st = ex(jnp.exp(a_last - a_cum))
    chunk_decay = ex(jnp.broadcast_to(jnp.exp(a_last), (SUBLANES, LANES)))[0:1, :]
    xdt_b = xdt.astype(BF16)
    wx_b = (decay_st * xdt).astype(BF16)

    heads_per_group = SSM_HEADS // SSM_GROUPS
    gw = heads_per_group * SSM_HEAD_DIM
    y_parts = []
    for g in range(SSM_GROUPS):
        bg = bm[:, g * SSM_STATE:(g + 1) * SSM_STATE]
        cg = cm[:, g * SSM_STATE:(g + 1) * SSM_STATE].astype(BF16)
        cb = lax.dot_general(cg, bg.astype(BF16), (((1,), (1,)), ((), ())), preferred_element_type=F32)
        yd = []
        for e in range(heads_per_group):
            h = g * heads_per_group + e
            seg = a_cum[:, h:h + 1] - a_cumT[h:h + 1, :]
            m = (cb * jnp.exp(jnp.where(causal, seg, -jnp.inf))).astype(BF16)
            yd.append(jnp.dot(m, xdt_b[:, h * SSM_HEAD_DIM:(h + 1) * SSM_HEAD_DIM], preferred_element_type=F32))
        yd = jnp.concatenate(yd, axis=1)
        st = st_ref[g]
        y_off = jnp.dot(cg, st.astype(BF16), preferred_element_type=F32) * decay_out[:, g * gw:(g + 1) * gw]
        y_parts.append(yd + y_off)
        bgT = jnp.transpose(bg).astype(BF16)
        st_ref[g] = st * chunk_decay[:, g * gw:(g + 1) * gw] + jnp.dot(
            bgT, wx_b[:, g * gw:(g + 1) * gw], preferred_element_type=F32)
    y = jnp.concatenate(y_parts, axis=1)
    d_e = ex(jnp.broadcast_to(dskip_ref[...], (SUBLANES, LANES)))[0:1, :]
    y = y + d_e * xs
    zz = z_ref[...]
    y = y * (zz * _sigmoid(zz))
    y_ref[...] = _rms_norm(y, g_ref[...]).astype(BF16)


def _ssd(z, xbc, dt, dtT, conv_w, conv_b, dt_bias, a_log, d_skip, norm_g, batch, seq):
    T = z.shape[0]
    nc = seq // SSM_CHUNK
    pad = lambda v: jnp.pad(v.reshape(1, -1), ((0, 0), (0, LANES - v.shape[-1])))
    colv = lambda v: v.reshape(-1, 1)
    head_of_lane = np.arange(SSM_WIDTH) // SSM_HEAD_DIM
    expand = jnp.asarray((np.arange(LANES)[:, None] == head_of_lane[None, :]).astype(np.float32)).astype(BF16)
    row = lambda w: pl.BlockSpec((SSM_CHUNK, w), lambda b, c: (b * nc + c, 0))
    full = lambda a: pl.BlockSpec(a.shape, lambda b, c: (0,) * a.ndim)
    args = (z, xbc, dt, dtT, conv_w, conv_b.reshape(1, -1), pad(dt_bias), colv(dt_bias), pad(a_log), colv(a_log),
            pad(d_skip), norm_g.reshape(1, -1), expand)
    in_specs = [row(SSM_WIDTH), row(SSM_CONV_WIDTH), row(LANES),
                pl.BlockSpec((None, SSM_HEADS, SSM_CHUNK), lambda b, c: (b, 0, c))]
    in_specs += [full(a) for a in args[4:]]
    return pl.pallas_call(
        _ssd_kernel,
        grid=(batch, nc),
        in_specs=in_specs,
        out_specs=row(SSM_WIDTH),
        out_shape=jax.ShapeDtypeStruct((T, SSM_WIDTH), BF16),
        scratch_shapes=[pltpu.VMEM((SUBLANES + SSM_CHUNK, SSM_CONV_WIDTH), F32),
                        pltpu.VMEM((SSM_GROUPS, SSM_STATE, SSM_WIDTH // SSM_GROUPS), F32)],
        compiler_params=_params(2),
        name="ssd",
    )(*args)


def _mlaproj_kernel(ql_ref, kvl_ref, kpe_ref, cos_ref, sin_ref, qg_ref, kvg_ref, wqn_ref, wqp_ref, wqs_ref,
                    wkn_ref, wv_ref, q_ref, k_ref, v_ref):
    scale = (MLA_NOPE + MLA_ROPE) ** -0.5
    cos = cos_ref[...]
    sin = sin_ref[...]
    qn = _rms_norm(ql_ref[...], qg_ref[...]).astype(BF16)
    q_nope = jnp.dot(qn, wqn_ref[...], preferred_element_type=F32) * scale
    q_pe = (jnp.dot(qn, wqp_ref[...], preferred_element_type=F32) * cos
            + jnp.dot(qn, wqs_ref[...], preferred_element_type=F32) * sin) * scale
    kvn = _rms_norm(kvl_ref[...], kvg_ref[...]).astype(BF16)
    k_nope = jnp.dot(kvn, wkn_ref[...], preferred_element_type=F32)
    v = jnp.dot(kvn, wv_ref[...], preferred_element_type=F32)
    kpe = kpe_ref[...]
    k_pe = kpe[:, :MLA_ROPE] * cos[:, :MLA_ROPE] + kpe[:, MLA_ROPE:] * sin[:, :MLA_ROPE]
    zpad = jnp.zeros((q_nope.shape[0], MXU_DEPTH - MLA_NOPE - MLA_ROPE), F32)
    for h in range(MLA_HEADS):
        q_ref[h] = jnp.concatenate(
            [q_nope[:, h * MLA_NOPE:(h + 1) * MLA_NOPE], q_pe[:, h * MLA_ROPE:(h + 1) * MLA_ROPE], zpad],
            axis=1).astype(BF16)
        k_ref[h] = jnp.concatenate([k_nope[:, h * MLA_NOPE:(h + 1) * MLA_NOPE], k_pe, zpad], axis=1).astype(BF16)
        v_ref[h] = v[:, h * MLA_V:(h + 1) * MLA_V].astype(BF16)


def _mlaproj(ql, kvl, kpe, cosq, sinq, q_g, kv_g, wqn, wqp, wqs, wkn, wv, batch, seq):
    T = ql.shape[0]
    per_seq = seq // ROW_TILE
    row = lambda w: pl.BlockSpec((ROW_TILE, w), lambda i: (i, 0))
    pos = lambda w: pl.BlockSpec((ROW_TILE, w), lambda i: (i % per_seq, 0))
    full = lambda a: pl.BlockSpec(a.shape, lambda i: (0,) * a.ndim)
    head = lambda w: pl.BlockSpec((None, MLA_HEADS, ROW_TILE, w), lambda i: (i // per_seq, 0, i % per_seq, 0))
    consts = (q_g.reshape(1, -1), kv_g.reshape(1, -1), wqn, wqp, wqs, wkn, wv)
    return pl.pallas_call(
        _mlaproj_kernel,
        grid=(T // ROW_TILE,),
        in_specs=[row(MLA_Q_RANK), row(MLA_KV_RANK), row(LANES), pos(MLA_HEADS * MLA_ROPE), pos(MLA_HEADS * MLA_ROPE)]
        + [full(a) for a in consts],
        out_specs=[head(MXU_DEPTH), head(MXU_DEPTH), head(MLA_V)],
        out_shape=[jax.ShapeDtypeStruct((batch, MLA_HEADS, seq, MXU_DEPTH), BF16),
                   jax.ShapeDtypeStruct((batch, MLA_HEADS, seq, MXU_DEPTH), BF16),
                   jax.ShapeDtypeStruct((batch, MLA_HEADS, seq, MLA_V), BF16)],
        compiler_params=_params(1),
        name="mlaproj",
    )(ql, kvl, kpe, cosq, sinq, *consts)


def _flash_kernel(q_ref, k_ref, v_ref, o_ref, m_ref, l_ref, acc_ref):
    blk = ATTN_BLOCK
    halves = ATTN_Q_TILE // blk
    qi = pl.program_id(2)
    m_ref[...] = jnp.full(m_ref.shape, -jnp.inf, F32)
    l_ref[...] = jnp.zeros(l_ref.shape, F32)
    acc_ref[...] = jnp.zeros(acc_ref.shape, F32)
    below_diag = (lax.broadcasted_iota(jnp.int32, (blk, blk), 1) <= lax.broadcasted_iota(jnp.int32, (blk, blk), 0))

    def lane_tiles(x):
        return [x[:, j * LANES:(j + 1) * LANES] for j in range(x.shape[1] // LANES)]

    def step(half, kv_block, masked):
        rows = pl.ds(half * blk, blk)
        off = pl.multiple_of(kv_block * blk, blk)
        s = lax.dot_general(q_ref[rows, :], k_ref[pl.ds(off, blk), :], (((1,), (1,)), ((), ())),
                            preferred_element_type=F32)
        if masked:
            s = jnp.where(below_diag, s, -jnp.inf)
        m_prev = m_ref[rows, :]
        m_tile = functools.reduce(jnp.maximum, lane_tiles(s))
        m_new = jnp.maximum(m_prev, jnp.broadcast_to(jnp.max(m_tile, axis=-1, keepdims=True), (blk, LANES)))
        p = jnp.exp(s - jnp.tile(m_new, (1, blk // LANES)))
        alpha = jnp.exp(m_prev - m_new)
        p_tile = functools.reduce(lambda a, b: a + b, lane_tiles(p))
        l_ref[rows, :] = alpha * l_ref[rows, :] + jnp.broadcast_to(jnp.sum(p_tile, axis=-1, keepdims=True),
                                                                   (blk, LANES))
        acc_ref[rows, :] = alpha * acc_ref[rows, :] + jnp.dot(p.astype(BF16), v_ref[pl.ds(off, blk), :],
                                                              preferred_element_type=F32)
        m_ref[rows, :] = m_new

    def bulk(j, carry):
        for half in range(halves):
            step(half, j, False)
        return carry

    first = qi * halves
    lax.fori_loop(0, first, bulk, 0)
    for half in range(halves):
        for d in range(half):
            step(half, first + d, False)
        step(half, first + half, True)
    o_ref[...] = (acc_ref[...] / l_ref[...]).astype(BF16)


def _flash(q, k, v):
    batch, heads, seq, _ = q.shape
    qspec = pl.BlockSpec((None, None, ATTN_Q_TILE, MXU_DEPTH), lambda b, h, i: (b, h, i, 0))
    kspec = pl.BlockSpec((None, None, seq, MXU_DEPTH), lambda b, h, i: (b, h, 0, 0))
    vspec = pl.BlockSpec((None, None, seq, MLA_V), lambda b, h, i: (b, h, 0, 0))
    return pl.pallas_call(
        _flash_kernel,
        grid=(batch, heads, seq // ATTN_Q_TILE),
        in_specs=[qspec, kspec, vspec],
        out_specs=pl.BlockSpec((None, ATTN_Q_TILE, MLA_V), lambda b, h, i: (b, i, h)),
        out_shape=jax.ShapeDtypeStruct((batch, seq, heads * MLA_V), BF16),
        scratch_shapes=[pltpu.VMEM((ATTN_Q_TILE, LANES), F32), pltpu.VMEM((ATTN_Q_TILE, LANES), F32),
                        pltpu.VMEM((ATTN_Q_TILE, MLA_V), F32)],
        compiler_params=_params(3),
        name="flash",
    )(q, k, v)


def _outproj_kernel(x_ref, ys_ref, ym_ref, g0_ref, b0_ref, wa_ref, wb_ref, g1_ref, b1_ref, o_ref):
    h = _layer_norm(x_ref[...], g0_ref[...], b0_ref[...])
    mix = (jnp.dot(ys_ref[...], wa_ref[...], preferred_element_type=F32)
           + jnp.dot(ym_ref[...], wb_ref[...], preferred_element_type=F32))
    o_ref[...] = _layer_norm(ALPHA * h + mix, g1_ref[...], b1_ref[...])


def _outproj(x2, y_ssm, y_mla, g0, b0, wa, wb, g1, b1):
    T = x2.shape[0]
    row = lambda w: pl.BlockSpec((ROW_TILE, w), lambda i: (i, 0))
    full = lambda a: pl.BlockSpec(a.shape, lambda i: (0,) * a.ndim)
    consts = (g0, b0, wa, wb, g1, b1)
    return pl.pallas_call(
        _outproj_kernel,
        grid=(T // ROW_TILE,),
        in_specs=[row(D_MODEL), row(SSM_WIDTH), row(MLA_WIDTH)] + [full(a) for a in consts],
        out_specs=row(D_MODEL),
        out_shape=jax.ShapeDtypeStruct((T, D_MODEL), F32),
        compiler_params=_params(1),
        name="outproj",
    )(x2, y_ssm, y_mla, *consts)


def _top_rows(s, k, payload=None):
    n = s.shape[0]
    rows = lax.broadcasted_iota(jnp.int32, s.shape, 0).astype(F32)
    vals, ids = [], []
    for _ in range(k):
        m = jnp.max(s, axis=0, keepdims=True)
        first = jnp.min(jnp.where(s == m, rows, float(n)), axis=0, keepdims=True)
        hit = rows == first
        vals.append(m)
        ids.append(first if payload is None else jnp.sum(jnp.where(hit, payload, 0.0), axis=0, keepdims=True))
        s = jnp.where(hit, -jnp.inf, s)
    return vals, ids


def _pair_candidates(v1, v2, combine):
    K = PEER_TOPK
    v1s = jnp.concatenate(v1, axis=0)
    v2s = jnp.concatenate(v2, axis=0)
    pieces = [combine(v1[0], v2s)]
    pieces += [combine(v1[a], v2s[:SUBLANES]) for a in range(1, SUBLANES)]
    pieces.append(combine(v1s[SUBLANES:], v2[0]))
    return jnp.concatenate(pieces, axis=0)


def _route_kernel(h_ref, wq_ref, keys_ref, idx_ref, gate_ref, off_scr, gate_scr):
    hb = h_ref[...].astype(BF16)
    K = PEER_TOPK
    for h in range(PEER_HEADS):
        tops = []
        for half in range(2):
            c0 = (h * 2 + half) * PEER_HALF
            q = jnp.dot(hb, wq_ref[:, c0:c0 + PEER_HALF], preferred_element_type=F32)
            sT = lax.dot_general(keys_ref[h * 2 + half], q.astype(BF16), (((1,), (1,)), ((), ())),
                                 preferred_element_type=F32)
            tops.append(_top_rows(sT, K))
        (v1, i1), (v2, i2) = tops
        cand_s = _pair_candidates(v1, v2, lambda x, y: x + y)
        cand_e = _pair_candidates(i1, i2, lambda x, y: x * float(PEER_KEYS) + y)
        top_s, top_e = _top_rows(cand_s, K, payload=cand_e)
        ex = [jnp.exp(s - top_s[0]) for s in top_s]
        denom = ex[0]
        for e in ex[1:]:
            denom = denom + e
        inv = 1.0 / denom
        for j in range(K):
            r = h * K + j
            off_scr[r:r + 1, :] = top_e[j] * float(WORDS_PER_ROW)
            gate_scr[r:r + 1, :] = ex[j] * inv
    idx_ref[...] = jnp.transpose(off_scr[...]).astype(jnp.int32)
    gate_ref[...] = jnp.transpose(gate_scr[...])


def _route(h2, wq, keys):
    T = h2.shape[0]
    full = lambda a: pl.BlockSpec(a.shape, lambda i: (0,) * a.ndim)
    out = pl.BlockSpec((ROUTE_TILE, PEER_SEL), lambda i: (i, 0))
    return pl.pallas_call(
        _route_kernel,
        grid=(T // ROUTE_TILE,),
        in_specs=[pl.BlockSpec((ROUTE_TILE, D_MODEL), lambda i: (i, 0)), full(wq), full(keys)],
        out_specs=[out, out],
        out_shape=[jax.ShapeDtypeStruct((T, PEER_SEL), jnp.int32), jax.ShapeDtypeStruct((T, PEER_SEL), F32)],
        scratch_shapes=[pltpu.VMEM((PEER_SEL, ROUTE_TILE), F32), pltpu.VMEM((PEER_SEL, ROUTE_TILE), F32)],
        compiler_params=_params(1),
        name="route",
    )(h2, wq, keys)


def _gather_group(idx_ref, tbl_ref, tile_ref, g):
    S = TILE_STRIDE
    for q in range(GATHER_GROUP):
        for kk in range(PEER_SEL):
            i = pl.multiple_of(idx_ref[g * GATHER_GROUP + q, kk], WORDS_PER_ROW)
            tile_ref[pl.ds(q * WORDS_PER_ROW * S + kk, WORDS_PER_ROW, stride=S), :] = tbl_ref[pl.ds(i, WORDS_PER_ROW), :]


def _tile_chunk(tile_ref, q, j):
    S = TILE_STRIDE
    return pltpu.bitcast(tile_ref[pl.ds(q * WORDS_PER_ROW * S + j * S, PEER_SEL), :], BF16)


def _pipelined_groups(idx_ref, tbl_ref, tile_a, tile_b, contract):
    groups = GATHER_TILE // GATHER_GROUP
    _gather_group(idx_ref, tbl_ref, tile_a, 0)

    def pair(p, carry):
        g = 2 * p
        _gather_group(idx_ref, tbl_ref, tile_b, g + 1)
        contract(tile_a, g)
        _gather_group(idx_ref, tbl_ref, tile_a, jnp.minimum(g + 2, groups - 1))
        contract(tile_b, g + 1)
        return carry

    lax.fori_loop(0, groups // 2, pair, 0)


def _keep_own_rows(acc, q, lo, hi):
    own = lax.broadcasted_iota(jnp.int32, lo.shape, 0) == q
    return jnp.where(own, acc[:GATHER_GROUP], lo), jnp.where(own, acc[GATHER_GROUP:], hi)


def _peer_u_kernel(idx_ref, x_ref, tbl_ref, lo_ref, hi_ref, tile_a, tile_b):
    half = D_MODEL // 2

    def contract(tile_ref, g):
        r0 = pl.multiple_of(GATHER_GROUP * g, GATHER_GROUP)
        x = x_ref[pl.ds(r0, GATHER_GROUP), :]
        lhs = jnp.concatenate([x[:, :half], x[:, half:]], axis=0).astype(BF16)
        lo = jnp.zeros((GATHER_GROUP, 2 * PEER_SEL), F32)
        hi = jnp.zeros((GATHER_GROUP, 2 * PEER_SEL), F32)
        for q in range(GATHER_GROUP):
            acc = jnp.zeros((2 * GATHER_GROUP, 2 * PEER_SEL), F32)
            for j in range(WORDS_PER_ROW):
                acc = acc + lax.dot_general(lhs[:, j * LANES:(j + 1) * LANES], _tile_chunk(tile_ref, q, j),
                                            (((1,), (1,)), ((), ())), preferred_element_type=F32)
            lo, hi = _keep_own_rows(acc, q, lo, hi)
        lo_ref[pl.ds(r0, GATHER_GROUP), :] = lo
        hi_ref[pl.ds(r0, GATHER_GROUP), :] = hi

    _pipelined_groups(idx_ref, tbl_ref, tile_a, tile_b, contract)


def _peer_v_kernel(idx_ref, wlo_ref, whi_ref, tbl_ref, o_ref, tile_a, tile_b):
    half = D_MODEL // 2

    def contract(tile_ref, g):
        r0 = pl.multiple_of(GATHER_GROUP * g, GATHER_GROUP)
        rows = pl.ds(r0, GATHER_GROUP)
        lhs = jnp.concatenate([wlo_ref[rows, :], whi_ref[rows, :]], axis=0).astype(BF16)
        for j in range(WORDS_PER_ROW):
            lo = jnp.zeros((GATHER_GROUP, LANES), F32)
            hi = jnp.zeros((GATHER_GROUP, LANES), F32)
            for q in range(GATHER_GROUP):
                r = jnp.dot(lhs, _tile_chunk(tile_ref, q, j), preferred_element_type=F32)
                lo, hi = _keep_own_rows(r, q, lo, hi)
            o_ref[rows, j * LANES:(j + 1) * LANES] = lo
            o_ref[rows, half + j * LANES:half + (j + 1) * LANES] = hi

    _pipelined_groups(idx_ref, tbl_ref, tile_a, tile_b, contract)


def _peer_call(kernel, name, idx, row_inputs, table, out_widths):
    T = idx.shape[0]
    tile = pltpu.VMEM((GATHER_GROUP * WORDS_PER_ROW * TILE_STRIDE, LANES), jnp.int32)
    row = lambda w: pl.BlockSpec((GATHER_TILE, w), lambda i: (i, 0))
    return pl.pallas_call(
        kernel,
        grid=(T // GATHER_TILE,),
        in_specs=[pl.BlockSpec((GATHER_TILE, PEER_SEL), lambda i: (i, 0), memory_space=pltpu.SMEM)]
        + [row(a.shape[1]) for a in row_inputs] + [pl.BlockSpec(memory_space=pltpu.VMEM)],
        out_specs=[row(w) for w in out_widths],
        out_shape=[jax.ShapeDtypeStruct((T, w), F32) for w in out_widths],
        scratch_shapes=[tile, tile],
        compiler_params=pltpu.CompilerParams(dimension_semantics=("arbitrary",), vmem_limit_bytes=VMEM_LIMIT_TABLE),
        name=name,
    )(idx, *row_inputs, table)


def _peer_w_kernel(lo_ref, hi_ref, gate_ref, sel_e_ref, sel_o_ref, wlo_ref, whi_ref):
    pick = (((1,), (1,)), ((), ()))
    act = _dot_f32_by_01(lo_ref[...], sel_e_ref[...], pick) + _dot_f32_by_01(hi_ref[...], sel_o_ref[...], pick)
    w = gate_ref[...] * (0.5 * act * (1.0 + lax.erf(act * (2.0 ** -0.5))))
    wlo_ref[...] = _dot_f32_by_01(w, sel_e_ref[...])
    whi_ref[...] = _dot_f32_by_01(w, sel_o_ref[...])


def _peer_w(act_lo, act_hi, gate):
    T = gate.shape[0]
    k = np.arange(PEER_SEL)
    sel_e = np.zeros((PEER_SEL, 2 * PEER_SEL), np.float32)
    sel_e[k, 2 * k] = 1.0
    sel_o = np.zeros((PEER_SEL, 2 * PEER_SEL), np.float32)
    sel_o[k, 2 * k + 1] = 1.0
    sel_e, sel_o = jnp.asarray(sel_e).astype(BF16), jnp.asarray(sel_o).astype(BF16)
    row = lambda w: pl.BlockSpec((ROW_TILE, w), lambda i: (i, 0))
    full = lambda a: pl.BlockSpec(a.shape, lambda i: (0,) * a.ndim)
    wide = 2 * PEER_SEL
    return pl.pallas_call(
        _peer_w_kernel,
        grid=(T // ROW_TILE,),
        in_specs=[row(wide), row(wide), row(PEER_SEL), full(sel_e), full(sel_o)],
        out_specs=[row(wide), row(wide)],
        out_shape=[jax.ShapeDtypeStruct((T, wide), F32), jax.ShapeDtypeStruct((T, wide), F32)],
        compiler_params=_params(1),
        name="peer_w",
    )(act_lo, act_hi, gate, sel_e, sel_o)


def _final_kernel(h_ref, f_ref, g_ref, b_ref, o_ref):
    o_ref[...] = _layer_norm(ALPHA * h_ref[...] + f_ref[...], g_ref[...], b_ref[...])


def _final(h2, ffn, g, b):
    T = h2.shape[0]
    row = pl.BlockSpec((ROW_TILE, D_MODEL), lambda i: (i, 0))
    full = lambda a: pl.BlockSpec(a.shape, lambda i: (0,) * a.ndim)
    return pl.pallas_call(
        _final_kernel,
        grid=(T // ROW_TILE,),
        in_specs=[row, row, full(g), full(b)],
        out_specs=row,
        out_shape=jax.ShapeDtypeStruct((T, D_MODEL), F32),
        compiler_params=_params(1),
        name="final",
    )(h2, ffn, g, b)


def _pack_table(tab):
    bits = lax.bitcast_convert_type(tab.astype(BF16), jnp.uint16).astype(jnp.uint32)
    half = D_MODEL // 2
    words = bits[:, :half] | (bits[:, half:] << 16)
    return lax.bitcast_convert_type(words, jnp.int32).reshape(-1, LANES)


def _rope_tables(seq):
    half = MLA_ROPE // 2
    inv_freq = ROPE_THETA ** (-jnp.arange(half, dtype=F32) / half)
    ang = jnp.arange(seq, dtype=F32)[:, None] * inv_freq
    cos, sin = jnp.cos(ang), jnp.sin(ang)
    cos64 = jnp.concatenate([cos, cos], axis=1)
    sin64 = jnp.concatenate([-sin, sin], axis=1)
    return jnp.tile(cos64, (1, MLA_HEADS)), jnp.tile(sin64, (1, MLA_HEADS))


def _swap_halves(w):
    half = w.shape[-1] // 2
    return jnp.concatenate([w[..., half:], w[..., :half]], axis=-1)


def kernel(x, ln_in_g, ln_in_b, w_in, conv_w, conv_b, dt_bias, a_log, d_skip, ssm_norm_g, q_norm_g, w_q_b, kv_norm_g, w_kv_b, w_out, ln_mix_g, ln_mix_b, w_query, sub_keys, u_table, v_table, ln_ffn_g, ln_ffn_b):
    batch, seq, d = x.shape
    assert d == D_MODEL and w_in.shape[0] == DEPTH == 1
    assert seq % ROW_TILE == 0 and seq % SSM_CHUNK == 0 and seq % ATTN_Q_TILE == 0
    T = batch * seq
    assert T % GATHER_TILE == 0 and T % ROUTE_TILE == 0
    x2 = x.reshape(T, d)
    vec = lambda v: v.reshape(1, -1)

    wi = w_in[0]
    o = np.cumsum((0, SSM_WIDTH, SSM_CONV_WIDTH, SSM_HEADS, MLA_Q_RANK, MLA_KV_RANK, MLA_ROPE))
    w_kpe = wi[:, o[5]:o[6]]
    w_packed = jnp.concatenate(
        [wi[:, o[0]:o[2]], wi[:, o[3]:o[5]], w_kpe, _swap_halves(w_kpe), wi[:, o[2]:o[3]],
         jnp.zeros((d, LANES - SSM_HEADS), F32)], axis=1).astype(BF16)
    z, xbc, ql, kvl, kpe, dt = _inproj(x2, vec(ln_in_g), vec(ln_in_b), w_packed)

    dtT = jnp.transpose(dt[:, :SSM_HEADS].reshape(batch, seq, SSM_HEADS), (0, 2, 1))
    y_ssm = _ssd(z, xbc, dt, dtT, conv_w[0], conv_b[0], dt_bias[0], a_log[0], d_skip[0], ssm_norm_g[0], batch, seq)

    wq = w_q_b[0].reshape(MLA_Q_RANK, MLA_HEADS, MLA_NOPE + MLA_ROPE)
    wq_pe = wq[:, :, MLA_NOPE:]
    wkv = w_kv_b[0].reshape(MLA_KV_RANK, MLA_HEADS, MLA_NOPE + MLA_V)
    flat = lambda w: w.reshape(w.shape[0], -1).astype(BF16)
    cosq, sinq = _rope_tables(seq)
    q, k, v = _mlaproj(ql, kvl, kpe, cosq, sinq, q_norm_g[0], kv_norm_g[0], flat(wq[:, :, :MLA_NOPE]), flat(wq_pe),
                       flat(_swap_halves(wq_pe)), flat(wkv[:, :, :MLA_NOPE]), flat(wkv[:, :, MLA_NOPE:]), batch, seq)
    y_mla = _flash(q, k, v).reshape(T, MLA_WIDTH)

    wo = w_out[0].astype(BF16)
    h2 = _outproj(x2, y_ssm, y_mla, vec(ln_in_g), vec(ln_in_b), wo[:SSM_WIDTH], wo[SSM_WIDTH:],
                  vec(ln_mix_g[0]), vec(ln_mix_b[0]))

    keys = sub_keys[0].reshape(PEER_HEADS * 2, PEER_KEYS, PEER_HALF).astype(BF16)
    idx, gate = _route(h2, w_query[0].astype(BF16), keys)

    wide = 2 * PEER_SEL
    act_lo, act_hi = _peer_call(_peer_u_kernel, "peer_u", idx, (h2,), _pack_table(u_table[0]), (wide, wide))
    w_lo, w_hi = _peer_w(act_lo, act_hi, gate)
    (ffn,) = _peer_call(_peer_v_kernel, "peer_v", idx, (w_lo, w_hi), _pack_table(v_table[0]), (D_MODEL,))
    out = _final(h2, ffn, vec(ln_ffn_g[0]), vec(ln_ffn_b[0]))
    return out.reshape(batch, seq, d)
```

```python
import functools

import numpy as np
import jax
import jax.numpy as jnp
from jax import lax
from jax.experimental import pallas as pl
from jax.experimental.pallas import tpu as pltpu

F32 = jnp.float32
BF16 = jnp.bfloat16

D_MODEL = 1024
SSM_HEADS = 16
SSM_HEAD_DIM = 64
SSM_WIDTH = SSM_HEADS * SSM_HEAD_DIM
SSM_GROUPS = 2
SSM_STATE = 128
SSM_CONV = 4
SSM_CHUNK = 256
SSM_CONV_WIDTH = SSM_WIDTH + 2 * SSM_GROUPS * SSM_STATE
MLA_HEADS = 8
MLA_Q_RANK = 384
MLA_KV_RANK = 256
MLA_NOPE = 128
MLA_ROPE = 64
MLA_V = 128
MLA_WIDTH = MLA_HEADS * MLA_V
ROPE_THETA = 10000.0
PEER_HEADS = 8
PEER_KEYS = 128
PEER_HALF = 128
PEER_TOPK = 16
PEER_SEL = PEER_HEADS * PEER_TOPK
DEPTH = 1
ALPHA = (2.0 * DEPTH) ** 0.25
EPS = 1e-5

LANES = 128
SUBLANES = 8
MXU_DEPTH = 256
VMEM_LIMIT = 48 * 1024 * 1024
VMEM_LIMIT_TABLE = 56 * 1024 * 1024

ROW_TILE = 512
ATTN_BLOCK = 1024
ATTN_Q_TILE = 2 * ATTN_BLOCK
ROUTE_TILE = 256
GATHER_TILE = 512
GATHER_GROUP = SUBLANES
GROUPS_PER_PASS = 2
TILE_STRIDE = PEER_SEL + 1
WORDS_PER_ROW = D_MODEL // 2 // LANES

_OFF_Z = 0
_OFF_XBC = _OFF_Z + SSM_WIDTH
_OFF_QL = _OFF_XBC + SSM_CONV_WIDTH
_OFF_KVL = _OFF_QL + MLA_Q_RANK
_OFF_KPE = _OFF_KVL + MLA_KV_RANK
_OFF_DT = _OFF_KPE + LANES
_IN_COLS = _OFF_DT + LANES


def _params(n_grid):
    return pltpu.CompilerParams(dimension_semantics=("arbitrary",) * n_grid, vmem_limit_bytes=VMEM_LIMIT)


def _layer_norm(x, g, b):
    mu = jnp.mean(x, axis=-1, keepdims=True)
    xc = x - mu
    var = jnp.mean(xc * xc, axis=-1, keepdims=True)
    return xc * lax.rsqrt(var + EPS) * g + b


def _rms_norm(x, g):
    return x * lax.rsqrt(jnp.mean(x * x, axis=-1, keepdims=True) + EPS) * g


def _sigmoid(x):
    return 1.0 / (1.0 + jnp.exp(-x))


def _softplus(x):
    return jnp.maximum(x, 0.0) + jnp.log1p(jnp.exp(-jnp.abs(x)))


def _bf16_pieces(a):
    p0 = a.astype(BF16)
    r = a - p0.astype(F32)
    p1 = r.astype(BF16)
    return p0, p1, (r - p1.astype(F32)).astype(BF16)


def _dot_f32_by_01(a, sel, dims=(((1,), (0,)), ((), ()))):
    out = None
    for piece in _bf16_pieces(a):
        t = lax.dot_general(piece, sel, dims, preferred_element_type=F32)
        out = t if out is None else out + t
    return out


def _dot_01_by_f32(sel, a):
    out = None
    for piece in _bf16_pieces(a):
        t = jnp.dot(sel, piece, preferred_element_type=F32)
        out = t if out is None else out + t
    return out


def _inproj_kernel(x_ref, g_ref, b_ref, w_ref, z_ref, xbc_ref, ql_ref, kvl_ref, kpe_ref, dt_ref):
    h = _layer_norm(x_ref[...], g_ref[...], b_ref[...]).astype(BF16)

    def seg(lo, hi):
        return jnp.dot(h, w_ref[:, lo:hi], preferred_element_type=F32)

    z_ref[...] = seg(_OFF_Z, _OFF_XBC)
    xbc_ref[...] = seg(_OFF_XBC, _OFF_QL)
    ql_ref[...] = seg(_OFF_QL, _OFF_KVL)
    kvl_ref[...] = seg(_OFF_KVL, _OFF_KPE)
    kpe_ref[...] = seg(_OFF_KPE, _OFF_DT)
    dt_ref[...] = seg(_OFF_DT, _IN_COLS)


def _inproj(x2, ln_g, ln_b, w_packed):
    T = x2.shape[0]
    widths = (SSM_WIDTH, SSM_CONV_WIDTH, MLA_Q_RANK, MLA_KV_RANK, LANES, LANES)
    row = lambda w: pl.BlockSpec((ROW_TILE, w), lambda i: (i, 0))
    full = lambda a: pl.BlockSpec(a.shape, lambda i: (0,) * a.ndim)
    return pl.pallas_call(
        _inproj_kernel,
        grid=(T // ROW_TILE,),
        in_specs=[row(D_MODEL), full(ln_g), full(ln_b), full(w_packed)],
        out_specs=[row(w) for w in widths],
        out_shape=[jax.ShapeDtypeStruct((T, w), F32) for w in widths],
        compiler_params=_params(1),
        name="inproj",
    )(x2, ln_g, ln_b, w_packed)


def _ssd_kernel(z_ref, xbc_ref, dt_ref, dtT_ref, convw_ref, convb_ref, dtb_ref, dtbT_ref, alog_ref, alogT_ref,
                dskip_ref, g_ref, e_ref, y_ref, xpad_ref, st_ref):
    Q = SSM_CHUNK
    HALO = SUBLANES
    c = pl.program_id(1)

    @pl.when(c == 0)
    def _():
        xpad_ref[0:HALO, :] = jnp.zeros((HALO, SSM_CONV_WIDTH), F32)
        st_ref[...] = jnp.zeros(st_ref.shape, F32)

    raw = xbc_ref[...]
    xpad_ref[HALO:HALO + Q, :] = raw
    conv = convb_ref[...] + raw * convw_ref[SSM_CONV - 1:SSM_CONV, :]
    for k in range(SSM_CONV - 1):
        off = HALO - (SSM_CONV - 1) + k
        conv = conv + xpad_ref[off:off + Q, :] * convw_ref[k:k + 1, :]
    xpad_ref[0:HALO, :] = raw[Q - HALO:Q, :]
    xbc = conv * _sigmoid(conv)
    xs = xbc[:, :SSM_WIDTH]
    bm = xbc[:, SSM_WIDTH:SSM_WIDTH + SSM_GROUPS * SSM_STATE]
    cm = xbc[:, SSM_WIDTH + SSM_GROUPS * SSM_STATE:]

    dt = _softplus(dt_ref[...] + dtb_ref[...])
    dtT = _softplus(dtT_ref[...] + dtbT_ref[...])
    a = dt * (-jnp.exp(alog_ref[...]))
    aT = dtT * (-jnp.exp(alogT_ref[...]))
    row = lax.broadcasted_iota(jnp.int32, (Q, Q), 0)
    col = lax.broadcasted_iota(jnp.int32, (Q, Q), 1)
    causal = col <= row
    lower = jnp.where(causal, 1.0, 0.0).astype(BF16)
    upper = jnp.where(row <= col, 1.0, 0.0).astype(BF16)
    a_cum = _dot_01_by_f32(lower, a)
    a_cumT = _dot_f32_by_01(aT, upper)
    a_last = a_cum[Q - 1:Q, :]

    expand = e_ref[...]
    ex = lambda v: _dot_f32_by_01(v, expand)
    xdt = xs * ex(dt)
    decay_out = ex(jnp.exp(a_cum))
    decay_st = ex(jnp.exp(a_last - a_cum))
    chunk_decay = ex(jnp.broadcast_to(jnp.exp(a_last), (SUBLANES, LANES)))[0:1, :]
    xdt_b = xdt.astype(BF16)
    wx_b = (decay_st * xdt).astype(BF16)

    heads_per_group = SSM_HEADS // SSM_GROUPS
    gw = heads_per_group * SSM_HEAD_DIM
    y_parts = []
    for g in range(SSM_GROUPS):
        bg = bm[:, g * SSM_STATE:(g + 1) * SSM_STATE]
        cg = cm[:, g * SSM_STATE:(g + 1) * SSM_STATE].astype(BF16)
        cb = lax.dot_general(cg, bg.astype(BF16), (((1,), (1,)), ((), ())), preferred_element_type=F32)
        yd = []
        for e in range(heads_per_group):
            h = g * heads_per_group + e
            seg = a_cum[:, h:h + 1] - a_cumT[h:h + 1, :]
            m = (cb * jnp.exp(jnp.where(causal, seg, -jnp.inf))).astype(BF16)
            yd.append(jnp.dot(m, xdt_b[:, h * SSM_HEAD_DIM:(h + 1) * SSM_HEAD_DIM], preferred_element_type=F32))
        yd = jnp.concatenate(yd, axis=1)
        st = st_ref[g]
        y_off = jnp.dot(cg, st.astype(BF16), preferred_element_type=F32) * decay_out[:, g * gw:(g + 1) * gw]
        y_parts.append(yd + y_off)
        bgT = jnp.transpose(bg).astype(BF16)
        st_ref[g] = st * chunk_decay[:, g * gw:(g + 1) * gw] + jnp.dot(
            bgT, wx_b[:, g * gw:(g + 1) * gw], preferred_element_type=F32)
    y = jnp.concatenate(y_parts, axis=1)
    d_e = ex(jnp.broadcast_to(dskip_ref[...], (SUBLANES, LANES)))[0:1, :]
    y = y + d_e * xs
    zz = z_ref[...]
    y = y * (zz * _sigmoid(zz))
    y_ref[...] = _rms_norm(y, g_ref[...]).astype(BF16)


def _ssd(z, xbc, dt, dtT, conv_w, conv_b, dt_bias, a_log, d_skip, norm_g, batch, seq):
    T = z.shape[0]
    nc = seq // SSM_CHUNK
    pad = lambda v: jnp.pad(v.reshape(1, -1), ((0, 0), (0, LANES - v.shape[-1])))
    colv = lambda v: v.reshape(-1, 1)
    head_of_lane = np.arange(SSM_WIDTH) // SSM_HEAD_DIM
    expand = jnp.asarray((np.arange(LANES)[:, None] == head_of_lane[None, :]).astype(np.float32)).astype(BF16)
    row = lambda w: pl.BlockSpec((SSM_CHUNK, w), lambda b, c: (b * nc + c, 0))
    full = lambda a: pl.BlockSpec(a.shape, lambda b, c: (0,) * a.ndim)
    args = (z, xbc, dt, dtT, conv_w, conv_b.reshape(1, -1), pad(dt_bias), colv(dt_bias), pad(a_log), colv(a_log),
            pad(d_skip), norm_g.reshape(1, -1), expand)
    in_specs = [row(SSM_WIDTH), row(SSM_CONV_WIDTH), row(LANES),
                pl.BlockSpec((None, SSM_HEADS, SSM_CHUNK), lambda b, c: (b, 0, c))]
    in_specs += [full(a) for a in args[4:]]
    return pl.pallas_call(
        _ssd_kernel,
        grid=(batch, nc),
        in_specs=in_specs,
        out_specs=row(SSM_WIDTH),
        out_shape=jax.ShapeDtypeStruct((T, SSM_WIDTH), BF16),
        scratch_shapes=[pltpu.VMEM((SUBLANES + SSM_CHUNK, SSM_CONV_WIDTH), F32),
                        pltpu.VMEM((SSM_GROUPS, SSM_STATE, SSM_WIDTH // SSM_GROUPS), F32)],
        compiler_params=_params(2),
        name="ssd",
    )(*args)


def _mlaproj_kernel(ql_ref, kvl_ref, kpe_ref, cos_ref, sin_ref, qg_ref, kvg_ref, wqn_ref, wqp_ref, wqs_ref,
                    wkn_ref, wv_ref, q_ref, k_ref, v_ref):
    scale = (MLA_NOPE + MLA_ROPE) ** -0.5
    cos = cos_ref[...]
    sin = sin_ref[...]
    qn = _rms_norm(ql_ref[...], qg_ref[...]).astype(BF16)
    q_nope = jnp.dot(qn, wqn_ref[...], preferred_element_type=F32) * scale
    q_pe = (jnp.dot(qn, wqp_ref[...], preferred_element_type=F32) * cos
            + jnp.dot(qn, wqs_ref[...], preferred_element_type=F32) * sin) * scale
    kvn = _rms_norm(kvl_ref[...], kvg_ref[...]).astype(BF16)
    k_nope = jnp.dot(kvn, wkn_ref[...], preferred_element_type=F32)
    v = jnp.dot(kvn, wv_ref[...], preferred_element_type=F32)
    kpe = kpe_ref[...]
    k_pe = kpe[:, :MLA_ROPE] * cos[:, :MLA_ROPE] + kpe[:, MLA_ROPE:] * sin[:, :MLA_ROPE]
    zpad = jnp.zeros((q_nope.shape[0], MXU_DEPTH - MLA_NOPE - MLA_ROPE), F32)
    for h in range(MLA_HEADS):
        q_ref[h] = jnp.concatenate(
            [q_nope[:, h * MLA_NOPE:(h + 1) * MLA_NOPE], q_pe[:, h * MLA_ROPE:(h + 1) * MLA_ROPE], zpad],
            axis=1).astype(BF16)
        k_ref[h] = jnp.concatenate([k_nope[:, h * MLA_NOPE:(h + 1) * MLA_NOPE], k_pe, zpad], axis=1).astype(BF16)
        v_ref[h] = v[:, h * MLA_V:(h + 1) * MLA_V].astype(BF16)


def _mlaproj(ql, kvl, kpe, cosq, sinq, q_g, kv_g, wqn, wqp, wqs, wkn, wv, batch, seq):
    T = ql.shape[0]
    per_seq = seq // ROW_TILE
    row = lambda w: pl.BlockSpec((ROW_TILE, w), lambda i: (i, 0))
    pos = lambda w: pl.BlockSpec((ROW_TILE, w), lambda i: (i % per_seq, 0))
    full = lambda a: pl.BlockSpec(a.shape, lambda i: (0,) * a.ndim)
    head = lambda w: pl.BlockSpec((None, MLA_HEADS, ROW_TILE, w), lambda i: (i // per_seq, 0, i % per_seq, 0))
    consts = (q_g.reshape(1, -1), kv_g.reshape(1, -1), wqn, wqp, wqs, wkn, wv)
    return pl.pallas_call(
        _mlaproj_kernel,
        grid=(T // ROW_TILE,),
        in_specs=[row(MLA_Q_RANK), row(MLA_KV_RANK), row(LANES), pos(MLA_HEADS * MLA_ROPE), pos(MLA_HEADS * MLA_ROPE)]
        + [full(a) for a in consts],
        out_specs=[head(MXU_DEPTH), head(MXU_DEPTH), head(MLA_V)],
        out_shape=[jax.ShapeDtypeStruct((batch, MLA_HEADS, seq, MXU_DEPTH), BF16),
                   jax.ShapeDtypeStruct((batch, MLA_HEADS, seq, MXU_DEPTH), BF16),
                   jax.ShapeDtypeStruct((batch, MLA_HEADS, seq, MLA_V), BF16)],
        compiler_params=_params(1),
        name="mlaproj",
    )(ql, kvl, kpe, cosq, sinq, *consts)


def _flash_kernel(q_ref, k_ref, v_ref, o_ref, m_ref, l_ref, acc_ref):
    blk = ATTN_BLOCK
    halves = ATTN_Q_TILE // blk
    qi = pl.program_id(2)
    m_ref[...] = jnp.full(m_ref.shape, -jnp.inf, F32)
    l_ref[...] = jnp.zeros(l_ref.shape, F32)
    acc_ref[...] = jnp.zeros(acc_ref.shape, F32)
    below_diag = (lax.broadcasted_iota(jnp.int32, (blk, blk), 1) <= lax.broadcasted_iota(jnp.int32, (blk, blk), 0))

    def lane_tiles(x):
        return [x[:, j * LANES:(j + 1) * LANES] for j in range(x.shape[1] // LANES)]

    def step(half, kv_block, masked):
        rows = pl.ds(half * blk, blk)
        off = pl.multiple_of(kv_block * blk, blk)
        s = lax.dot_general(q_ref[rows, :], k_ref[pl.ds(off, blk), :], (((1,), (1,)), ((), ())),
                            preferred_element_type=F32)
        if masked:
            s = jnp.where(below_diag, s, -jnp.inf)
        m_prev = m_ref[rows, :]
        m_tile = functools.reduce(jnp.maximum, lane_tiles(s))
        m_new = jnp.maximum(m_prev, jnp.broadcast_to(jnp.max(m_tile, axis=-1, keepdims=True), (blk, LANES)))
        p = jnp.exp(s - jnp.tile(m_new, (1, blk // LANES)))
        alpha = jnp.exp(m_prev - m_new)
        p_tile = functools.reduce(lambda a, b: a + b, lane_tiles(p))
        l_ref[rows, :] = alpha * l_ref[rows, :] + jnp.broadcast_to(jnp.sum(p_tile, axis=-1, keepdims=True),
                                                                   (blk, LANES))
        acc_ref[rows, :] = alpha * acc_ref[rows, :] + jnp.dot(p.astype(BF16), v_ref[pl.ds(off, blk), :],
                                                              preferred_element_type=F32)
        m_ref[rows, :] = m_new

    def bulk(j, carry):
        for half in range(halves):
            step(half, j, False)
        return carry

    first = qi * halves
    lax.fori_loop(0, first, bulk, 0)
    for half in range(halves):
        for d in range(half):
            step(half, first + d, False)
        step(half, first + half, True)
    o_ref[...] = (acc_ref[...] / l_ref[...]).astype(BF16)


def _flash(q, k, v):
    batch, heads, seq, _ = q.shape
    qspec = pl.BlockSpec((None, None, ATTN_Q_TILE, MXU_DEPTH), lambda b, h, i: (b, h, i, 0))
    kspec = pl.BlockSpec((None, None, seq, MXU_DEPTH), lambda b, h, i: (b, h, 0, 0))
    vspec = pl.BlockSpec((None, None, seq, MLA_V), lambda b, h, i: (b, h, 0, 0))
    return pl.pallas_call(
        _flash_kernel,
        grid=(batch, heads, seq // ATTN_Q_TILE),
        in_specs=[qspec, kspec, vspec],
        out_specs=pl.BlockSpec((None, ATTN_Q_TILE, MLA_V), lambda b, h, i: (b, i, h)),
        out_shape=jax.ShapeDtypeStruct((batch, seq, heads * MLA_V), BF16),
        scratch_shapes=[pltpu.VMEM((ATTN_Q_TILE, LANES), F32), pltpu.VMEM((ATTN_Q_TILE, LANES), F32),
                        pltpu.VMEM((ATTN_Q_TILE, MLA_V), F32)],
        compiler_params=_params(3),
        name="flash",
    )(q, k, v)


def _outproj_kernel(x_ref, ys_ref, ym_ref, g0_ref, b0_ref, wa_ref, wb_ref, g1_ref, b1_ref, o_ref):
    h = _layer_norm(x_ref[...], g0_ref[...], b0_ref[...])
    mix = (jnp.dot(ys_ref[...], wa_ref[...], preferred_element_type=F32)
           + jnp.dot(ym_ref[...], wb_ref[...], preferred_element_type=F32))
    o_ref[...] = _layer_norm(ALPHA * h + mix, g1_ref[...], b1_ref[...])


def _outproj(x2, y_ssm, y_mla, g0, b0, wa, wb, g1, b1):
    T = x2.shape[0]
    row = lambda w: pl.BlockSpec((ROW_TILE, w), lambda i: (i, 0))
    full = lambda a: pl.BlockSpec(a.shape, lambda i: (0,) * a.ndim)
    consts = (g0, b0, wa, wb, g1, b1)
    return pl.pallas_call(
        _outproj_kernel,
        grid=(T // ROW_TILE,),
        in_specs=[row(D_MODEL), row(SSM_WIDTH), row(MLA_WIDTH)] + [full(a) for a in consts],
        out_specs=row(D_MODEL),
        out_shape=jax.ShapeDtypeStruct((T, D_MODEL), F32),
        compiler_params=_params(1),
        name="outproj",
    )(x2, y_ssm, y_mla, *consts)


def _top_rows(s, k, payload=None):
    n = s.shape[0]
    rows = lax.broadcasted_iota(jnp.int32, s.shape, 0).astype(F32)
    vals, ids = [], []
    for _ in range(k):
        m = jnp.max(s, axis=0, keepdims=True)
        first = jnp.min(jnp.where(s == m, rows, float(n)), axis=0, keepdims=True)
        hit = rows == first
        vals.append(m)
        ids.append(first if payload is None else jnp.sum(jnp.where(hit, payload, 0.0), axis=0, keepdims=True))
        s = jnp.where(hit, -jnp.inf, s)
    return vals, ids


def _pair_candidates(v1, v2, combine):
    K = PEER_TOPK
    v1s = jnp.concatenate(v1, axis=0)
    v2s = jnp.concatenate(v2, axis=0)
    pieces = [combine(v1[0], v2s)]
    pieces += [combine(v1[a], v2s[:SUBLANES]) for a in range(1, SUBLANES)]
    pieces.append(combine(v1s[SUBLANES:], v2[0]))
    return jnp.concatenate(pieces, axis=0)


def _route_kernel(h_ref, wq_ref, keys_ref, idx_ref, gate_ref, off_scr, gate_scr):
    hb = h_ref[...].astype(BF16)
    K = PEER_TOPK
    for h in range(PEER_HEADS):
        tops = []
        for half in range(2):
            c0 = (h * 2 + half) * PEER_HALF
            q = jnp.dot(hb, wq_ref[:, c0:c0 + PEER_HALF], preferred_element_type=F32)
            sT = lax.dot_general(keys_ref[h * 2 + half], q.astype(BF16), (((1,), (1,)), ((), ())),
                                 preferred_element_type=F32)
            tops.append(_top_rows(sT, K))
        (v1, i1), (v2, i2) = tops
        cand_s = _pair_candidates(v1, v2, lambda x, y: x + y)
        cand_e = _pair_candidates(i1, i2, lambda x, y: x * float(PEER_KEYS) + y)
        top_s, top_e = _top_rows(cand_s, K, payload=cand_e)
        ex = [jnp.exp(s - top_s[0]) for s in top_s]
        denom = ex[0]
        for e in ex[1:]:
            denom = denom + e
        inv = 1.0 / denom
        for j in range(K):
            r = h * K + j
            off_scr[r:r + 1, :] = top_e[j] * float(WORDS_PER_ROW)
            gate_scr[r:r + 1, :] = ex[j] * inv
    idx_ref[...] = jnp.transpose(off_scr[...]).astype(jnp.int32)
    gate_ref[...] = jnp.transpose(gate_scr[...])


def _route(h2, wq, keys):
    T = h2.shape[0]
    full = lambda a: pl.BlockSpec(a.shape, lambda i: (0,) * a.ndim)
    out = pl.BlockSpec((ROUTE_TILE, PEER_SEL), lambda i: (i, 0))
    return pl.pallas_call(
        _route_kernel,
        grid=(T // ROUTE_TILE,),
        in_specs=[pl.BlockSpec((ROUTE_TILE, D_MODEL), lambda i: (i, 0)), full(wq), full(keys)],
        out_specs=[out, out],
        out_shape=[jax.ShapeDtypeStruct((T, PEER_SEL), jnp.int32), jax.ShapeDtypeStruct((T, PEER_SEL), F32)],
        scratch_shapes=[pltpu.VMEM((PEER_SEL, ROUTE_TILE), F32), pltpu.VMEM((PEER_SEL, ROUTE_TILE), F32)],
        compiler_params=_params(1),
        name="route",
    )(h2, wq, keys)


def _gather_group(idx_ref, tbl_ref, tile_ref, g):
    S = TILE_STRIDE
    for q in range(GATHER_GROUP):
        for kk in range(PEER_SEL):
            i = pl.multiple_of(idx_ref[g * GATHER_GROUP + q, kk], WORDS_PER_ROW)
            tile_ref[pl.ds(q * WORDS_PER_ROW * S + kk, WORDS_PER_ROW, stride=S), :] = tbl_ref[pl.ds(i, WORDS_PER_ROW), :]


def _tile_chunk(tile_ref, q, j):
    S = TILE_STRIDE
    return pltpu.bitcast(tile_ref[pl.ds(q * WORDS_PER_ROW * S + j * S, PEER_SEL), :], BF16)


def _pipelined_groups(idx_ref, tbl_ref, tile_a, tile_b, contract):
    groups = GATHER_TILE // GATHER_GROUP
    tiles = (tile_a, tile_b)
    _gather_group(idx_ref, tbl_ref, tile_a, 0)

    def several(p, carry):
        for k in range(GROUPS_PER_PASS):
            g = GROUPS_PER_PASS * p + k
            nxt = g + 1 if k + 1 < GROUPS_PER_PASS else jnp.minimum(g + 1, groups - 1)
            _gather_group(idx_ref, tbl_ref, tiles[(k + 1) % 2], nxt)
            contract(tiles[k % 2], g)
        return carry

    lax.fori_loop(0, groups // GROUPS_PER_PASS, several, 0)


def _keep_own_rows(acc, q, lo, hi):
    own = lax.broadcasted_iota(jnp.int32, lo.shape, 0) == q
    return jnp.where(own, acc[:GATHER_GROUP], lo), jnp.where(own, acc[GATHER_GROUP:], hi)


def _peer_u_kernel(idx_ref, x_ref, tbl_ref, lo_ref, hi_ref, tile_a, tile_b):
    half = D_MODEL // 2

    def contract(tile_ref, g):
        r0 = pl.multiple_of(GATHER_GROUP * g, GATHER_GROUP)
        x = x_ref[pl.ds(r0, GATHER_GROUP), :]
        lhs = jnp.concatenate([x[:, :half], x[:, half:]], axis=0).astype(BF16)
        lo = jnp.zeros((GATHER_GROUP, 2 * PEER_SEL), F32)
        hi = jnp.zeros((GATHER_GROUP, 2 * PEER_SEL), F32)
        for q in range(GATHER_GROUP):
            acc = jnp.zeros((2 * GATHER_GROUP, 2 * PEER_SEL), F32)
            for j in range(0, WORDS_PER_ROW, 2):
                rows2 = jnp.concatenate([_tile_chunk(tile_ref, q, j), _tile_chunk(tile_ref, q, j + 1)], axis=1)
                acc = acc + lax.dot_general(lhs[:, j * LANES:(j + 2) * LANES], rows2,
                                            (((1,), (1,)), ((), ())), preferred_element_type=F32)
            lo, hi = _keep_own_rows(acc, q, lo, hi)
        lo_ref[pl.ds(r0, GATHER_GROUP), :] = lo
        hi_ref[pl.ds(r0, GATHER_GROUP), :] = hi

    _pipelined_groups(idx_ref, tbl_ref, tile_a, tile_b, contract)


def _peer_v_kernel(idx_ref, wlo_ref, whi_ref, tbl_ref, o_ref, tile_a, tile_b):
    half = D_MODEL // 2

    def contract(tile_ref, g):
        r0 = pl.multiple_of(GATHER_GROUP * g, GATHER_GROUP)
        rows = pl.ds(r0, GATHER_GROUP)
        lhs = jnp.concatenate([wlo_ref[rows, :], whi_ref[rows, :]], axis=0).astype(BF16)
        lo = [jnp.zeros((GATHER_GROUP, LANES), F32)] * WORDS_PER_ROW
        hi = [jnp.zeros((GATHER_GROUP, LANES), F32)] * WORDS_PER_ROW
        for q in range(GATHER_GROUP):
            for j in range(WORDS_PER_ROW):
                r = jnp.dot(lhs, _tile_chunk(tile_ref, q, j), preferred_element_type=F32)
                lo[j], hi[j] = _keep_own_rows(r, q, lo[j], hi[j])
        o_ref[rows, :] = jnp.concatenate(lo + hi, axis=1)

    _pipelined_groups(idx_ref, tbl_ref, tile_a, tile_b, contract)


def _peer_call(kernel, name, idx, row_inputs, table, out_widths):
    T = idx.shape[0]
    tile = pltpu.VMEM((GATHER_GROUP * WORDS_PER_ROW * TILE_STRIDE, LANES), jnp.int32)
    row = lambda w: pl.BlockSpec((GATHER_TILE, w), lambda i: (i, 0))
    return pl.pallas_call(
        kernel,
        grid=(T // GATHER_TILE,),
        in_specs=[pl.BlockSpec((GATHER_TILE, PEER_SEL), lambda i: (i, 0), memory_space=pltpu.SMEM)]
        + [row(a.shape[1]) for a in row_inputs] + [pl.BlockSpec(memory_space=pltpu.VMEM)],
        out_specs=[row(w) for w in out_widths],
        out_shape=[jax.ShapeDtypeStruct((T, w), F32) for w in out_widths],
        scratch_shapes=[tile, tile],
        compiler_params=pltpu.CompilerParams(dimension_semantics=("arbitrary",), vmem_limit_bytes=VMEM_LIMIT_TABLE),
        name=name,
    )(idx, *row_inputs, table)


def _peer_w_kernel(lo_ref, hi_ref, gate_ref, sel_e_ref, sel_o_ref, wlo_ref, whi_ref):
    pick = (((1,), (1,)), ((), ()))
    act = _dot_f32_by_01(lo_ref[...], sel_e_ref[...], pick) + _dot_f32_by_01(hi_ref[...], sel_o_ref[...], pick)
    w = gate_ref[...] * (0.5 * act * (1.0 + lax.erf(act * (2.0 ** -0.5))))
    wlo_ref[...] = _dot_f32_by_01(w, sel_e_ref[...])
    whi_ref[...] = _dot_f32_by_01(w, sel_o_ref[...])


def _peer_w(act_lo, act_hi, gate):
    T = gate.shape[0]
    k = np.arange(PEER_SEL)
    sel_e = np.zeros((PEER_SEL, 2 * PEER_SEL), np.float32)
    sel_e[k, 2 * k] = 1.0
    sel_o = np.zeros((PEER_SEL, 2 * PEER_SEL), np.float32)
    sel_o[k, 2 * k + 1] = 1.0
    sel_e, sel_o = jnp.asarray(sel_e).astype(BF16), jnp.asarray(sel_o).astype(BF16)
    row = lambda w: pl.BlockSpec((ROW_TILE, w), lambda i: (i, 0))
    full = lambda a: pl.BlockSpec(a.shape, lambda i: (0,) * a.ndim)
    wide = 2 * PEER_SEL
    return pl.pallas_call(
        _peer_w_kernel,
        grid=(T // ROW_TILE,),
        in_specs=[row(wide), row(wide), row(PEER_SEL), full(sel_e), full(sel_o)],
        out_specs=[row(wide), row(wide)],
        out_shape=[jax.ShapeDtypeStruct((T, wide), F32), jax.ShapeDtypeStruct((T, wide), F32)],
        compiler_params=_params(1),
        name="peer_w",
    )(act_lo, act_hi, gate, sel_e, sel_o)


def _final_kernel(h_ref, f_ref, g_ref, b_ref, o_ref):
    o_ref[...] = _layer_norm(ALPHA * h_ref[...] + f_ref[...], g_ref[...], b_ref[...])


def _final(h2, ffn, g, b):
    T = h2.shape[0]
    row = pl.BlockSpec((ROW_TILE, D_MODEL), lambda i: (i, 0))
    full = lambda a: pl.BlockSpec(a.shape, lambda i: (0,) * a.ndim)
    return pl.pallas_call(
        _final_kernel,
        grid=(T // ROW_TILE,),
        in_specs=[row, row, full(g), full(b)],
        out_specs=row,
        out_shape=jax.ShapeDtypeStruct((T, D_MODEL), F32),
        compiler_params=_params(1),
        name="final",
    )(h2, ffn, g, b)


def _pack_table(tab):
    bits = lax.bitcast_convert_type(tab.astype(BF16), jnp.uint16).astype(jnp.uint32)
    half = D_MODEL // 2
    words = bits[:, :half] | (bits[:, half:] << 16)
    return lax.bitcast_convert_type(words, jnp.int32).reshape(-1, LANES)


def _rope_tables(seq):
    half = MLA_ROPE // 2
    inv_freq = ROPE_THETA ** (-jnp.arange(half, dtype=F32) / half)
    ang = jnp.arange(seq, dtype=F32)[:, None] * inv_freq
    cos, sin = jnp.cos(ang), jnp.sin(ang)
    cos64 = jnp.concatenate([cos, cos], axis=1)
    sin64 = jnp.concatenate([-sin, sin], axis=1)
    return jnp.tile(cos64, (1, MLA_HEADS)), jnp.tile(sin64, (1, MLA_HEADS))


def _swap_halves(w):
    half = w.shape[-1] // 2
    return jnp.concatenate([w[..., half:], w[..., :half]], axis=-1)


def kernel(x, ln_in_g, ln_in_b, w_in, conv_w, conv_b, dt_bias, a_log, d_skip, ssm_norm_g, q_norm_g, w_q_b, kv_norm_g, w_kv_b, w_out, ln_mix_g, ln_mix_b, w_query, sub_keys, u_table, v_table, ln_ffn_g, ln_ffn_b):
    batch, seq, d = x.shape
    assert d == D_MODEL and w_in.shape[0] == DEPTH == 1
    assert seq % ROW_TILE == 0 and seq % SSM_CHUNK == 0 and seq % ATTN_Q_TILE == 0
    T = batch * seq
    assert T % GATHER_TILE == 0 and T % ROUTE_TILE == 0
    x2 = x.reshape(T, d)
    vec = lambda v: v.reshape(1, -1)

    wi = w_in[0]
    o = np.cumsum((0, SSM_WIDTH, SSM_CONV_WIDTH, SSM_HEADS, MLA_Q_RANK, MLA_KV_RANK, MLA_ROPE))
    w_kpe = wi[:, o[5]:o[6]]
    w_packed = jnp.concatenate(
        [wi[:, o[0]:o[2]], wi[:, o[3]:o[5]], w_kpe, _swap_halves(w_kpe), wi[:, o[2]:o[3]],
         jnp.zeros((d, LANES - SSM_HEADS), F32)], axis=1).astype(BF16)
    z, xbc, ql, kvl, kpe, dt = _inproj(x2, vec(ln_in_g), vec(ln_in_b), w_packed)

    dtT = jnp.transpose(dt[:, :SSM_HEADS].reshape(batch, seq, SSM_HEADS), (0, 2, 1))
    y_ssm = _ssd(z, xbc, dt, dtT, conv_w[0], conv_b[0], dt_bias[0], a_log[0], d_skip[0], ssm_norm_g[0], batch, seq)

    wq = w_q_b[0].reshape(MLA_Q_RANK, MLA_HEADS, MLA_NOPE + MLA_ROPE)
    wq_pe = wq[:, :, MLA_NOPE:]
    wkv = w_kv_b[0].reshape(MLA_KV_RANK, MLA_HEADS, MLA_NOPE + MLA_V)
    flat = lambda w: w.reshape(w.shape[0], -1).astype(BF16)
    cosq, sinq = _rope_tables(seq)
    q, k, v = _mlaproj(ql, kvl, kpe, cosq, sinq, q_norm_g[0], kv_norm_g[0], flat(wq[:, :, :MLA_NOPE]), flat(wq_pe),
                       flat(_swap_halves(wq_pe)), flat(wkv[:, :, :MLA_NOPE]), flat(wkv[:, :, MLA_NOPE:]), batch, seq)
    y_mla = _flash(q, k, v).reshape(T, MLA_WIDTH)

    wo = w_out[0].astype(BF16)
    h2 = _outproj(x2, y_ssm, y_mla, vec(ln_in_g), vec(ln_in_b), wo[:SSM_WIDTH], wo[SSM_WIDTH:],
                  vec(ln_mix_g[0]), vec(ln_mix_b[0]))

    keys = sub_keys[0].reshape(PEER_HEADS * 2, PEER_KEYS, PEER_HALF).astype(BF16)
    idx, gate = _route(h2, w_query[0].astype(BF16), keys)

    wide = 2 * PEER_SEL
    act_lo, act_hi = _peer_call(_peer_u_kernel, "peer_u", idx, (h2,), _pack_table(u_table[0]), (wide, wide))
    w_lo, w_hi = _peer_w(act_lo, act_hi, gate)
    (ffn,) = _peer_call(_peer_v_kernel, "peer_v", idx, (w_lo, w_hi), _pack_table(v_table[0]), (D_MODEL,))
    out = _final(h2, ffn, vec(ln_ffn_g[0]), vec(ln_ffn_b[0]))
    return out.reshape(batch, seq, d)
```

```python
import functools

import numpy as np
import jax
import jax.numpy as jnp
from jax import lax
from jax.experimental import pallas as pl
from jax.experimental.pallas import tpu as pltpu

F32 = jnp.float32
BF16 = jnp.bfloat16

D_MODEL = 1024
SSM_HEADS = 16
SSM_HEAD_DIM = 64
SSM_WIDTH = SSM_HEADS * SSM_HEAD_DIM
SSM_GROUPS = 2
SSM_STATE = 128
SSM_CONV = 4
SSM_CHUNK = 256
SSM_CONV_WIDTH = SSM_WIDTH + 2 * SSM_GROUPS * SSM_STATE
MLA_HEADS = 8
MLA_Q_RANK = 384
MLA_KV_RANK = 256
MLA_NOPE = 128
MLA_ROPE = 64
MLA_V = 128
MLA_WIDTH = MLA_HEADS * MLA_V
ROPE_THETA = 10000.0
PEER_HEADS = 8
PEER_KEYS = 128
PEER_HALF = 128
PEER_TOPK = 16
PEER_SEL = PEER_HEADS * PEER_TOPK
DEPTH = 1
ALPHA = (2.0 * DEPTH) ** 0.25
EPS = 1e-5

LANES = 128
SUBLANES = 8
MXU_DEPTH = 256
VMEM_LIMIT = 48 * 1024 * 1024
VMEM_LIMIT_TABLE = 56 * 1024 * 1024

ROW_TILE = 512
ATTN_BLOCK = 1024
ATTN_Q_TILE = 4 * ATTN_BLOCK
ROUTE_TILE = 512
GATHER_TILE = 512
GATHER_GROUP = SUBLANES
GROUPS_PER_PASS = 2
TILE_STRIDE = PEER_SEL + 1
WORDS_PER_ROW = D_MODEL // 2 // LANES

_OFF_Z = 0
_OFF_XBC = _OFF_Z + SSM_WIDTH
_OFF_QL = _OFF_XBC + SSM_CONV_WIDTH
_OFF_KVL = _OFF_QL + MLA_Q_RANK
_OFF_KPE = _OFF_KVL + MLA_KV_RANK
_OFF_DT = _OFF_KPE + LANES
_IN_COLS = _OFF_DT + LANES


def _params(n_grid):
    return pltpu.CompilerParams(dimension_semantics=("arbitrary",) * n_grid, vmem_limit_bytes=VMEM_LIMIT)


def _layer_norm(x, g, b):
    mu = jnp.mean(x, axis=-1, keepdims=True)
    xc = x - mu
    var = jnp.mean(xc * xc, axis=-1, keepdims=True)
    return xc * lax.rsqrt(var + EPS) * g + b


def _rms_norm(x, g):
    return x * lax.rsqrt(jnp.mean(x * x, axis=-1, keepdims=True) + EPS) * g


def _sigmoid(x):
    return 1.0 / (1.0 + jnp.exp(-x))


def _softplus(x):
    return jnp.maximum(x, 0.0) + jnp.log1p(jnp.exp(-jnp.abs(x)))


def _bf16_pieces(a):
    p0 = a.astype(BF16)
    r = a - p0.astype(F32)
    p1 = r.astype(BF16)
    return p0, p1, (r - p1.astype(F32)).astype(BF16)


def _dot_f32_by_01(a, sel, dims=(((1,), (0,)), ((), ()))):
    out = None
    for piece in _bf16_pieces(a):
        t = lax.dot_general(piece, sel, dims, preferred_element_type=F32)
        out = t if out is None else out + t
    return out


def _dot_01_by_f32(sel, a):
    out = None
    for piece in _bf16_pieces(a):
        t = jnp.dot(sel, piece, preferred_element_type=F32)
        out = t if out is None else out + t
    return out


def _inproj_kernel(x_ref, g_ref, b_ref, w_ref, z_ref, xbc_ref, ql_ref, kvl_ref, kpe_ref, dt_ref):
    h = _layer_norm(x_ref[...], g_ref[...], b_ref[...]).astype(BF16)

    def seg(lo, hi):
        return jnp.dot(h, w_ref[:, lo:hi], preferred_element_type=F32)

    z_ref[...] = seg(_OFF_Z, _OFF_XBC)
    xbc_ref[...] = seg(_OFF_XBC, _OFF_QL)
    ql_ref[...] = seg(_OFF_QL, _OFF_KVL)
    kvl_ref[...] = seg(_OFF_KVL, _OFF_KPE)
    kpe_ref[...] = seg(_OFF_KPE, _OFF_DT)
    dt_ref[...] = seg(_OFF_DT, _IN_COLS)


def _inproj(x2, ln_g, ln_b, w_packed):
    T = x2.shape[0]
    widths = (SSM_WIDTH, SSM_CONV_WIDTH, MLA_Q_RANK, MLA_KV_RANK, LANES, LANES)
    row = lambda w: pl.BlockSpec((ROW_TILE, w), lambda i: (i, 0))
    full = lambda a: pl.BlockSpec(a.shape, lambda i: (0,) * a.ndim)
    return pl.pallas_call(
        _inproj_kernel,
        grid=(T // ROW_TILE,),
        in_specs=[row(D_MODEL), full(ln_g), full(ln_b), full(w_packed)],
        out_specs=[row(w) for w in widths],
        out_shape=[jax.ShapeDtypeStruct((T, w), F32) for w in widths],
        compiler_params=_params(1),
        name="inproj",
    )(x2, ln_g, ln_b, w_packed)


def _ssd_kernel(z_ref, xbc_ref, dt_ref, dtT_ref, convw_ref, convb_ref, dtb_ref, dtbT_ref, alog_ref, alogT_ref,
                dskip_ref, g_ref, e_ref, y_ref, xpad_ref, st_ref):
    Q = SSM_CHUNK
    HALO = SUBLANES
    c = pl.program_id(1)

    @pl.when(c == 0)
    def _():
        xpad_ref[0:HALO, :] = jnp.zeros((HALO, SSM_CONV_WIDTH), F32)
        st_ref[...] = jnp.zeros(st_ref.shape, F32)

    raw = xbc_ref[...]
    xpad_ref[HALO:HALO + Q, :] = raw
    conv = convb_ref[...] + raw * convw_ref[SSM_CONV - 1:SSM_CONV, :]
    for k in range(SSM_CONV - 1):
        off = HALO - (SSM_CONV - 1) + k
        conv = conv + xpad_ref[off:off + Q, :] * convw_ref[k:k + 1, :]
    xpad_ref[0:HALO, :] = raw[Q - HALO:Q, :]
    xbc = conv * _sigmoid(conv)
    xs = xbc[:, :SSM_WIDTH]
    bm = xbc[:, SSM_WIDTH:SSM_WIDTH + SSM_GROUPS * SSM_STATE]
    cm = xbc[:, SSM_WIDTH + SSM_GROUPS * SSM_STATE:]

    dt = _softplus(dt_ref[...] + dtb_ref[...])
    dtT = _softplus(dtT_ref[...] + dtbT_ref[...])
    a = dt * (-jnp.exp(alog_ref[...]))
    aT = dtT * (-jnp.exp(alogT_ref[...]))
    row = lax.broadcasted_iota(jnp.int32, (Q, Q), 0)
    col = lax.broadcasted_iota(jnp.int32, (Q, Q), 1)
    causal = col <= row
    lower = jnp.where(causal, 1.0, 0.0).astype(BF16)
    upper = jnp.where(row <= col, 1.0, 0.0).astype(BF16)
    a_cum = _dot_01_by_f32(lower, a)
    a_cumT = _dot_f32_by_01(aT, upper)
    a_last = a_cum[Q - 1:Q, :]

    expand = e_ref[...]
    ex = lambda v: _dot_f32_by_01(v, expand)
    xdt = xs * ex(dt)
    decay_out = ex(jnp.exp(a_cum))
    decay_st = ex(jnp.exp(a_last - a_cum))
    chunk_decay = ex(jnp.broadcast_to(jnp.exp(a_last), (SUBLANES, LANES)))[0:1, :]
    xdt_b = xdt.astype(BF16)
    wx_b = (decay_st * xdt).astype(BF16)

    heads_per_group = SSM_HEADS // SSM_GROUPS
    gw = heads_per_group * SSM_HEAD_DIM
    y_parts = []
    for g in range(SSM_GROUPS):
        bg = bm[:, g * SSM_STATE:(g + 1) * SSM_STATE]
        cg = cm[:, g * SSM_STATE:(g + 1) * SSM_STATE].astype(BF16)
        cb = lax.dot_general(cg, bg.astype(BF16), (((1,), (1,)), ((), ())), preferred_element_type=F32)
        yd = []
        for e in range(heads_per_group):
            h = g * heads_per_group + e
            seg = a_cum[:, h:h + 1] - a_cumT[h:h + 1, :]
            m = (cb * jnp.exp(jnp.where(causal, seg, -jnp.inf))).astype(BF16)
            yd.append(jnp.dot(m, xdt_b[:, h * SSM_HEAD_DIM:(h + 1) * SSM_HEAD_DIM], preferred_element_type=F32))
        yd = jnp.concatenate(yd, axis=1)
        st = st_ref[g]
        y_off = jnp.dot(cg, st.astype(BF16), preferred_element_type=F32) * decay_out[:, g * gw:(g + 1) * gw]
        y_parts.append(yd + y_off)
        bgT = jnp.transpose(bg).astype(BF16)
        st_ref[g] = st * chunk_decay[:, g * gw:(g + 1) * gw] + jnp.dot(
            bgT, wx_b[:, g * gw:(g + 1) * gw], preferred_element_type=F32)
    y = jnp.concatenate(y_parts, axis=1)
    d_e = ex(jnp.broadcast_to(dskip_ref[...], (SUBLANES, LANES)))[0:1, :]
    y = y + d_e * xs
    zz = z_ref[...]
    y = y * (zz * _sigmoid(zz))
    y_ref[...] = _rms_norm(y, g_ref[...]).astype(BF16)


def _ssd(z, xbc, dt, dtT, conv_w, conv_b, dt_bias, a_log, d_skip, norm_g, batch, seq):
    T = z.shape[0]
    nc = seq // SSM_CHUNK
    pad = lambda v: jnp.pad(v.reshape(1, -1), ((0, 0), (0, LANES - v.shape[-1])))
    colv = lambda v: v.reshape(-1, 1)
    head_of_lane = np.arange(SSM_WIDTH) // SSM_HEAD_DIM
    expand = jnp.asarray((np.arange(LANES)[:, None] == head_of_lane[None, :]).astype(np.float32)).astype(BF16)
    row = lambda w: pl.BlockSpec((SSM_CHUNK, w), lambda b, c: (b * nc + c, 0))
    full = lambda a: pl.BlockSpec(a.shape, lambda b, c: (0,) * a.ndim)
    args = (z, xbc, dt, dtT, conv_w, conv_b.reshape(1, -1), pad(dt_bias), colv(dt_bias), pad(a_log), colv(a_log),
            pad(d_skip), norm_g.reshape(1, -1), expand)
    in_specs = [row(SSM_WIDTH), row(SSM_CONV_WIDTH), row(LANES),
                pl.BlockSpec((None, SSM_HEADS, SSM_CHUNK), lambda b, c: (b, 0, c))]
    in_specs += [full(a) for a in args[4:]]
    return pl.pallas_call(
        _ssd_kernel,
        grid=(batch, nc),
        in_specs=in_specs,
        out_specs=row(SSM_WIDTH),
        out_shape=jax.ShapeDtypeStruct((T, SSM_WIDTH), BF16),
        scratch_shapes=[pltpu.VMEM((SUBLANES + SSM_CHUNK, SSM_CONV_WIDTH), F32),
                        pltpu.VMEM((SSM_GROUPS, SSM_STATE, SSM_WIDTH // SSM_GROUPS), F32)],
        compiler_params=_params(2),
        name="ssd",
    )(*args)


def _mlaproj_kernel(ql_ref, kvl_ref, kpe_ref, cos_ref, sin_ref, qg_ref, kvg_ref, wqn_ref, wqp_ref, wqs_ref,
                    wkn_ref, wv_ref, q_ref, k_ref, v_ref):
    scale = (MLA_NOPE + MLA_ROPE) ** -0.5
    cos = cos_ref[...]
    sin = sin_ref[...]
    qn = _rms_norm(ql_ref[...], qg_ref[...]).astype(BF16)
    q_nope = jnp.dot(qn, wqn_ref[...], preferred_element_type=F32) * scale
    q_pe = (jnp.dot(qn, wqp_ref[...], preferred_element_type=F32) * cos
            + jnp.dot(qn, wqs_ref[...], preferred_element_type=F32) * sin) * scale
    kvn = _rms_norm(kvl_ref[...], kvg_ref[...]).astype(BF16)
    k_nope = jnp.dot(kvn, wkn_ref[...], preferred_element_type=F32)
    v = jnp.dot(kvn, wv_ref[...], preferred_element_type=F32)
    kpe = kpe_ref[...]
    k_pe = kpe[:, :MLA_ROPE] * cos[:, :MLA_ROPE] + kpe[:, MLA_ROPE:] * sin[:, :MLA_ROPE]
    zpad = jnp.zeros((q_nope.shape[0], MXU_DEPTH - MLA_NOPE - MLA_ROPE), F32)
    for h in range(MLA_HEADS):
        q_ref[h] = jnp.concatenate(
            [q_nope[:, h * MLA_NOPE:(h + 1) * MLA_NOPE], q_pe[:, h * MLA_ROPE:(h + 1) * MLA_ROPE], zpad],
            axis=1).astype(BF16)
        k_ref[h] = jnp.concatenate([k_nope[:, h * MLA_NOPE:(h + 1) * MLA_NOPE], k_pe, zpad], axis=1).astype(BF16)
        v_ref[h] = v[:, h * MLA_V:(h + 1) * MLA_V].astype(BF16)


def _mlaproj(ql, kvl, kpe, cosq, sinq, q_g, kv_g, wqn, wqp, wqs, wkn, wv, batch, seq):
    T = ql.shape[0]
    per_seq = seq // ROW_TILE
    row = lambda w: pl.BlockSpec((ROW_TILE, w), lambda i: (i, 0))
    pos = lambda w: pl.BlockSpec((ROW_TILE, w), lambda i: (i % per_seq, 0))
    full = lambda a: pl.BlockSpec(a.shape, lambda i: (0,) * a.ndim)
    head = lambda w: pl.BlockSpec((None, MLA_HEADS, ROW_TILE, w), lambda i: (i // per_seq, 0, i % per_seq, 0))
    consts = (q_g.reshape(1, -1), kv_g.reshape(1, -1), wqn, wqp, wqs, wkn, wv)
    return pl.pallas_call(
        _mlaproj_kernel,
        grid=(T // ROW_TILE,),
        in_specs=[row(MLA_Q_RANK), row(MLA_KV_RANK), row(LANES), pos(MLA_HEADS * MLA_ROPE), pos(MLA_HEADS * MLA_ROPE)]
        + [full(a) for a in consts],
        out_specs=[head(MXU_DEPTH), head(MXU_DEPTH), head(MLA_V)],
        out_shape=[jax.ShapeDtypeStruct((batch, MLA_HEADS, seq, MXU_DEPTH), BF16),
                   jax.ShapeDtypeStruct((batch, MLA_HEADS, seq, MXU_DEPTH), BF16),
                   jax.ShapeDtypeStruct((batch, MLA_HEADS, seq, MLA_V), BF16)],
        compiler_params=_params(1),
        name="mlaproj",
    )(ql, kvl, kpe, cosq, sinq, *consts)


def _flash_kernel(q_ref, k_ref, v_ref, o_ref, m_ref, l_ref, acc_ref):
    blk = ATTN_BLOCK
    halves = ATTN_Q_TILE // blk
    qi = pl.program_id(2)
    m_ref[...] = jnp.full(m_ref.shape, -jnp.inf, F32)
    l_ref[...] = jnp.zeros(l_ref.shape, F32)
    acc_ref[...] = jnp.zeros(acc_ref.shape, F32)
    below_diag = (lax.broadcasted_iota(jnp.int32, (blk, blk), 1) <= lax.broadcasted_iota(jnp.int32, (blk, blk), 0))

    def lane_tiles(x):
        return [x[:, j * LANES:(j + 1) * LANES] for j in range(x.shape[1] // LANES)]

    def step(half, kv_block, masked):
        rows = pl.ds(half * blk, blk)
        off = pl.multiple_of(kv_block * blk, blk)
        s = lax.dot_general(q_ref[rows, :], k_ref[pl.ds(off, blk), :], (((1,), (1,)), ((), ())),
                            preferred_element_type=F32)
        if masked:
            s = jnp.where(below_diag, s, -jnp.inf)
        m_prev = m_ref[rows, :]
        m_tile = functools.reduce(jnp.maximum, lane_tiles(s))
        m_new = jnp.maximum(m_prev, jnp.broadcast_to(jnp.max(m_tile, axis=-1, keepdims=True), (blk, LANES)))
        p = jnp.exp(s - jnp.tile(m_new, (1, blk // LANES)))
        alpha = jnp.exp(m_prev - m_new)
        p_tile = functools.reduce(lambda a, b: a + b, lane_tiles(p))
        l_ref[rows, :] = alpha * l_ref[rows, :] + jnp.broadcast_to(jnp.sum(p_tile, axis=-1, keepdims=True),
                                                                   (blk, LANES))
        acc_ref[rows, :] = alpha * acc_ref[rows, :] + jnp.dot(p.astype(BF16), v_ref[pl.ds(off, blk), :],
                                                              preferred_element_type=F32)
        m_ref[rows, :] = m_new

    def bulk(j, carry):
        for half in range(halves):
            step(half, j, False)
        return carry

    first = qi * halves
    lax.fori_loop(0, first, bulk, 0)
    for half in range(halves):
        for d in range(half):
            step(half, first + d, False)
        step(half, first + half, True)
    o_ref[...] = (acc_ref[...] / l_ref[...]).astype(BF16)


def _flash(q, k, v):
    batch, heads, seq, _ = q.shape
    qspec = pl.BlockSpec((None, None, ATTN_Q_TILE, MXU_DEPTH), lambda b, h, i: (b, h, i, 0))
    kspec = pl.BlockSpec((None, None, seq, MXU_DEPTH), lambda b, h, i: (b, h, 0, 0))
    vspec = pl.BlockSpec((None, None, seq, MLA_V), lambda b, h, i: (b, h, 0, 0))
    return pl.pallas_call(
        _flash_kernel,
        grid=(batch, heads, seq // ATTN_Q_TILE),
        in_specs=[qspec, kspec, vspec],
        out_specs=pl.BlockSpec((None, ATTN_Q_TILE, MLA_V), lambda b, h, i: (b, i, h)),
        out_shape=jax.ShapeDtypeStruct((batch, seq, heads * MLA_V), BF16),
        scratch_shapes=[pltpu.VMEM((ATTN_Q_TILE, LANES), F32), pltpu.VMEM((ATTN_Q_TILE, LANES), F32),
                        pltpu.VMEM((ATTN_Q_TILE, MLA_V), F32)],
        compiler_params=_params(3),
        name="flash",
    )(q, k, v)


def _outproj_kernel(x_ref, ys_ref, ym_ref, g0_ref, b0_ref, wa_ref, wb_ref, g1_ref, b1_ref, o_ref):
    h = _layer_norm(x_ref[...], g0_ref[...], b0_ref[...])
    mix = (jnp.dot(ys_ref[...], wa_ref[...], preferred_element_type=F32)
           + jnp.dot(ym_ref[...], wb_ref[...], preferred_element_type=F32))
    o_ref[...] = _layer_norm(ALPHA * h + mix, g1_ref[...], b1_ref[...])


def _outproj(x2, y_ssm, y_mla, g0, b0, wa, wb, g1, b1):
    T = x2.shape[0]
    row = lambda w: pl.BlockSpec((ROW_TILE, w), lambda i: (i, 0))
    full = lambda a: pl.BlockSpec(a.shape, lambda i: (0,) * a.ndim)
    consts = (g0, b0, wa, wb, g1, b1)
    return pl.pallas_call(
        _outproj_kernel,
        grid=(T // ROW_TILE,),
        in_specs=[row(D_MODEL), row(SSM_WIDTH), row(MLA_WIDTH)] + [full(a) for a in consts],
        out_specs=row(D_MODEL),
        out_shape=jax.ShapeDtypeStruct((T, D_MODEL), F32),
        compiler_params=_params(1),
        name="outproj",
    )(x2, y_ssm, y_mla, *consts)


def _top_rows(s, k, payload=None):
    n = s.shape[0]
    rows = lax.broadcasted_iota(jnp.int32, s.shape, 0).astype(F32)
    vals, ids = [], []
    for _ in range(k):
        m = jnp.max(s, axis=0, keepdims=True)
        first = jnp.min(jnp.where(s == m, rows, float(n)), axis=0, keepdims=True)
        hit = rows == first
        vals.append(m)
        ids.append(first if payload is None else jnp.sum(jnp.where(hit, payload, 0.0), axis=0, keepdims=True))
        s = jnp.where(hit, -jnp.inf, s)
    return vals, ids


def _pair_candidates(v1, v2, combine):
    K = PEER_TOPK
    v1s = jnp.concatenate(v1, axis=0)
    v2s = jnp.concatenate(v2, axis=0)
    pieces = [combine(v1[0], v2s)]
    pieces += [combine(v1[a], v2s[:SUBLANES]) for a in range(1, SUBLANES)]
    pieces.append(combine(v1s[SUBLANES:], v2[0]))
    return jnp.concatenate(pieces, axis=0)


def _route_kernel(h_ref, wq_ref, keys_ref, idx_ref, gate_ref, off_scr, gate_scr):
    hb = h_ref[...].astype(BF16)
    K = PEER_TOPK
    for h in range(PEER_HEADS):
        tops = []
        for half in range(2):
            c0 = (h * 2 + half) * PEER_HALF
            q = jnp.dot(hb, wq_ref[:, c0:c0 + PEER_HALF], preferred_element_type=F32)
            sT = lax.dot_general(keys_ref[h * 2 + half], q.astype(BF16), (((1,), (1,)), ((), ())),
                                 preferred_element_type=F32)
            tops.append(_top_rows(sT, K))
        (v1, i1), (v2, i2) = tops
        cand_s = _pair_candidates(v1, v2, lambda x, y: x + y)
        cand_e = _pair_candidates(i1, i2, lambda x, y: x * float(PEER_KEYS) + y)
        top_s, top_e = _top_rows(cand_s, K, payload=cand_e)
        ex = [jnp.exp(s - top_s[0]) for s in top_s]
        denom = ex[0]
        for e in ex[1:]:
            denom = denom + e
        inv = 1.0 / denom
        for j in range(K):
            r = h * K + j
            off_scr[r:r + 1, :] = top_e[j] * float(WORDS_PER_ROW)
            gate_scr[r:r + 1, :] = ex[j] * inv
    idx_ref[...] = jnp.transpose(off_scr[...]).astype(jnp.int32)
    gate_ref[...] = jnp.transpose(gate_scr[...])


def _route(h2, wq, keys):
    T = h2.shape[0]
    full = lambda a: pl.BlockSpec(a.shape, lambda i: (0,) * a.ndim)
    out = pl.BlockSpec((ROUTE_TILE, PEER_SEL), lambda i: (i, 0))
    return pl.pallas_call(
        _route_kernel,
        grid=(T // ROUTE_TILE,),
        in_specs=[pl.BlockSpec((ROUTE_TILE, D_MODEL), lambda i: (i, 0)), full(wq), full(keys)],
        out_specs=[out, out],
        out_shape=[jax.ShapeDtypeStruct((T, PEER_SEL), jnp.int32), jax.ShapeDtypeStruct((T, PEER_SEL), F32)],
        scratch_shapes=[pltpu.VMEM((PEER_SEL, ROUTE_TILE), F32), pltpu.VMEM((PEER_SEL, ROUTE_TILE), F32)],
        compiler_params=_params(1),
        name="route",
    )(h2, wq, keys)


def _gather_group(idx_ref, tbl_ref, tile_ref, g):
    S = TILE_STRIDE
    for q in range(GATHER_GROUP):
        for kk in range(PEER_SEL):
            i = pl.multiple_of(idx_ref[g * GATHER_GROUP + q, kk], WORDS_PER_ROW)
            tile_ref[pl.ds(q * WORDS_PER_ROW * S + kk, WORDS_PER_ROW, stride=S), :] = tbl_ref[pl.ds(i, WORDS_PER_ROW), :]


def _tile_chunk(tile_ref, q, j):
    S = TILE_STRIDE
    return pltpu.bitcast(tile_ref[pl.ds(q * WORDS_PER_ROW * S + j * S, PEER_SEL), :], BF16)


def _pipelined_groups(idx_ref, tbl_ref, tile_a, tile_b, contract):
    groups = GATHER_TILE // GATHER_GROUP
    tiles = (tile_a, tile_b)
    _gather_group(idx_ref, tbl_ref, tile_a, 0)

    def several(p, carry):
        for k in range(GROUPS_PER_PASS):
            g = GROUPS_PER_PASS * p + k
            nxt = g + 1 if k + 1 < GROUPS_PER_PASS else jnp.minimum(g + 1, groups - 1)
            _gather_group(idx_ref, tbl_ref, tiles[(k + 1) % 2], nxt)
            contract(tiles[k % 2], g)
        return carry

    lax.fori_loop(0, groups // GROUPS_PER_PASS, several, 0)


def _keep_own_rows(acc, q, lo, hi):
    own = lax.broadcasted_iota(jnp.int32, lo.shape, 0) == q
    return jnp.where(own, acc[:GATHER_GROUP], lo), jnp.where(own, acc[GATHER_GROUP:], hi)


def _peer_u_kernel(idx_ref, x_ref, tbl_ref, lo_ref, hi_ref, tile_a, tile_b):
    half = D_MODEL // 2

    def contract(tile_ref, g):
        r0 = pl.multiple_of(GATHER_GROUP * g, GATHER_GROUP)
        x = x_ref[pl.ds(r0, GATHER_GROUP), :]
        lhs = jnp.concatenate([x[:, :half], x[:, half:]], axis=0).astype(BF16)
        lo = jnp.zeros((GATHER_GROUP, 2 * PEER_SEL), F32)
        hi = jnp.zeros((GATHER_GROUP, 2 * PEER_SEL), F32)
        for q in range(GATHER_GROUP):
            acc = jnp.zeros((2 * GATHER_GROUP, 2 * PEER_SEL), F32)
            for j in range(0, WORDS_PER_ROW, 2):
                rows2 = jnp.concatenate([_tile_chunk(tile_ref, q, j), _tile_chunk(tile_ref, q, j + 1)], axis=1)
                acc = acc + lax.dot_general(lhs[:, j * LANES:(j + 2) * LANES], rows2,
                                            (((1,), (1,)), ((), ())), preferred_element_type=F32)
            lo, hi = _keep_own_rows(acc, q, lo, hi)
        lo_ref[pl.ds(r0, GATHER_GROUP), :] = lo
        hi_ref[pl.ds(r0, GATHER_GROUP), :] = hi

    _pipelined_groups(idx_ref, tbl_ref, tile_a, tile_b, contract)


def _peer_v_kernel(idx_ref, wlo_ref, whi_ref, tbl_ref, o_ref, tile_a, tile_b):
    half = D_MODEL // 2

    def contract(tile_ref, g):
        r0 = pl.multiple_of(GATHER_GROUP * g, GATHER_GROUP)
        rows = pl.ds(r0, GATHER_GROUP)
        lhs = jnp.concatenate([wlo_ref[rows, :], whi_ref[rows, :]], axis=0).astype(BF16)
        lo = [jnp.zeros((GATHER_GROUP, LANES), F32)] * WORDS_PER_ROW
        hi = [jnp.zeros((GATHER_GROUP, LANES), F32)] * WORDS_PER_ROW
        for q in range(GATHER_GROUP):
            for j in range(WORDS_PER_ROW):
                r = jnp.dot(lhs, _tile_chunk(tile_ref, q, j), preferred_element_type=F32)
                lo[j], hi[j] = _keep_own_rows(r, q, lo[j], hi[j])
        o_ref[rows, :] = jnp.concatenate(lo + hi, axis=1)

    _pipelined_groups(idx_ref, tbl_ref, tile_a, tile_b, contract)


def _peer_call(kernel, name, idx, row_inputs, table, out_widths):
    T = idx.shape[0]
    tile = pltpu.VMEM((GATHER_GROUP * WORDS_PER_ROW * TILE_STRIDE, LANES), jnp.int32)
    row = lambda w: pl.BlockSpec((GATHER_TILE, w), lambda i: (i, 0))
    return pl.pallas_call(
        kernel,
        grid=(T // GATHER_TILE,),
        in_specs=[pl.BlockSpec((GATHER_TILE, PEER_SEL), lambda i: (i, 0), memory_space=pltpu.SMEM)]
        + [row(a.shape[1]) for a in row_inputs] + [pl.BlockSpec(memory_space=pltpu.VMEM)],
        out_specs=[row(w) for w in out_widths],
        out_shape=[jax.ShapeDtypeStruct((T, w), F32) for w in out_widths],
        scratch_shapes=[tile, tile],
        compiler_params=pltpu.CompilerParams(dimension_semantics=("arbitrary",), vmem_limit_bytes=VMEM_LIMIT_TABLE),
        name=name,
    )(idx, *row_inputs, table)


def _peer_w_kernel(lo_ref, hi_ref, gate_ref, sel_e_ref, sel_o_ref, wlo_ref, whi_ref):
    pick = (((1,), (1,)), ((), ()))
    act = _dot_f32_by_01(lo_ref[...], sel_e_ref[...], pick) + _dot_f32_by_01(hi_ref[...], sel_o_ref[...], pick)
    w = gate_ref[...] * (0.5 * act * (1.0 + lax.erf(act * (2.0 ** -0.5))))
    wlo_ref[...] = _dot_f32_by_01(w, sel_e_ref[...])
    whi_ref[...] = _dot_f32_by_01(w, sel_o_ref[...])


def _peer_w(act_lo, act_hi, gate):
    T = gate.shape[0]
    k = np.arange(PEER_SEL)
    sel_e = np.zeros((PEER_SEL, 2 * PEER_SEL), np.float32)
    sel_e[k, 2 * k] = 1.0
    sel_o = np.zeros((PEER_SEL, 2 * PEER_SEL), np.float32)
    sel_o[k, 2 * k + 1] = 1.0
    sel_e, sel_o = jnp.asarray(sel_e).astype(BF16), jnp.asarray(sel_o).astype(BF16)
    row = lambda w: pl.BlockSpec((ROW_TILE, w), lambda i: (i, 0))
    full = lambda a: pl.BlockSpec(a.shape, lambda i: (0,) * a.ndim)
    wide = 2 * PEER_SEL
    return pl.pallas_call(
        _peer_w_kernel,
        grid=(T // ROW_TILE,),
        in_specs=[row(wide), row(wide), row(PEER_SEL), full(sel_e), full(sel_o)],
        out_specs=[row(wide), row(wide)],
        out_shape=[jax.ShapeDtypeStruct((T, wide), F32), jax.ShapeDtypeStruct((T, wide), F32)],
        compiler_params=_params(1),
        name="peer_w",
    )(act_lo, act_hi, gate, sel_e, sel_o)


def _final_kernel(h_ref, f_ref, g_ref, b_ref, o_ref):
    o_ref[...] = _layer_norm(ALPHA * h_ref[...] + f_ref[...], g_ref[...], b_ref[...])


def _final(h2, ffn, g, b):
    T = h2.shape[0]
    row = pl.BlockSpec((ROW_TILE, D_MODEL), lambda i: (i, 0))
    full = lambda a: pl.BlockSpec(a.shape, lambda i: (0,) * a.ndim)
    return pl.pallas_call(
        _final_kernel,
        grid=(T // ROW_TILE,),
        in_specs=[row, row, full(g), full(b)],
        out_specs=row,
        out_shape=jax.ShapeDtypeStruct((T, D_MODEL), F32),
        compiler_params=_params(1),
        name="final",
    )(h2, ffn, g, b)


def _pack_table(tab):
    bits = lax.bitcast_convert_type(tab.astype(BF16), jnp.uint16).astype(jnp.uint32)
    half = D_MODEL // 2
    words = bits[:, :half] | (bits[:, half:] << 16)
    return lax.bitcast_convert_type(words, jnp.int32).reshape(-1, LANES)


def _rope_tables(seq):
    half = MLA_ROPE // 2
    inv_freq = ROPE_THETA ** (-jnp.arange(half, dtype=F32) / half)
    ang = jnp.arange(seq, dtype=F32)[:, None] * inv_freq
    cos, sin = jnp.cos(ang), jnp.sin(ang)
    cos64 = jnp.concatenate([cos, cos], axis=1)
    sin64 = jnp.concatenate([-sin, sin], axis=1)
    return jnp.tile(cos64, (1, MLA_HEADS)), jnp.tile(sin64, (1, MLA_HEADS))


def _swap_halves(w):
    half = w.shape[-1] // 2
    return jnp.concatenate([w[..., half:], w[..., :half]], axis=-1)


def kernel(x, ln_in_g, ln_in_b, w_in, conv_w, conv_b, dt_bias, a_log, d_skip, ssm_norm_g, q_norm_g, w_q_b, kv_norm_g, w_kv_b, w_out, ln_mix_g, ln_mix_b, w_query, sub_keys, u_table, v_table, ln_ffn_g, ln_ffn_b):
    batch, seq, d = x.shape
    assert d == D_MODEL and w_in.shape[0] == DEPTH == 1
    assert seq % ROW_TILE == 0 and seq % SSM_CHUNK == 0 and seq % ATTN_Q_TILE == 0
    T = batch * seq
    assert T % GATHER_TILE == 0 and T % ROUTE_TILE == 0
    x2 = x.reshape(T, d)
    vec = lambda v: v.reshape(1, -1)

    wi = w_in[0]
    o = np.cumsum((0, SSM_WIDTH, SSM_CONV_WIDTH, SSM_HEADS, MLA_Q_RANK, MLA_KV_RANK, MLA_ROPE))
    w_kpe = wi[:, o[5]:o[6]]
    w_packed = jnp.concatenate(
        [wi[:, o[0]:o[2]], wi[:, o[3]:o[5]], w_kpe, _swap_halves(w_kpe), wi[:, o[2]:o[3]],
         jnp.zeros((d, LANES - SSM_HEADS), F32)], axis=1).astype(BF16)
    z, xbc, ql, kvl, kpe, dt = _inproj(x2, vec(ln_in_g), vec(ln_in_b), w_packed)

    dtT = jnp.transpose(dt[:, :SSM_HEADS].reshape(batch, seq, SSM_HEADS), (0, 2, 1))
    y_ssm = _ssd(z, xbc, dt, dtT, conv_w[0], conv_b[0], dt_bias[0], a_log[0], d_skip[0], ssm_norm_g[0], batch, seq)

    wq = w_q_b[0].reshape(MLA_Q_RANK, MLA_HEADS, MLA_NOPE + MLA_ROPE)
    wq_pe = wq[:, :, MLA_NOPE:]
    wkv = w_kv_b[0].reshape(MLA_KV_RANK, MLA_HEADS, MLA_NOPE + MLA_V)
    flat = lambda w: w.reshape(w.shape[0], -1).astype(BF16)
    cosq, sinq = _rope_tables(seq)
    q, k, v = _mlaproj(ql, kvl, kpe, cosq, sinq, q_norm_g[0], kv_norm_g[0], flat(wq[:, :, :MLA_NOPE]), flat(wq_pe),
                       flat(_swap_halves(wq_pe)), flat(wkv[:, :, :MLA_NOPE]), flat(wkv[:, :, MLA_NOPE:]), batch, seq)
    y_mla = _flash(q, k, v).reshape(T, MLA_WIDTH)

    wo = w_out[0].astype(BF16)
    h2 = _outproj(x2, y_ssm, y_mla, vec(ln_in_g), vec(ln_in_b), wo[:SSM_WIDTH], wo[SSM_WIDTH:],
                  vec(ln_mix_g[0]), vec(ln_mix_b[0]))

    keys = sub_keys[0].reshape(PEER_HEADS * 2, PEER_KEYS, PEER_HALF).astype(BF16)
    idx, gate = _route(h2, w_query[0].astype(BF16), keys)

    wide = 2 * PEER_SEL
    act_lo, act_hi = _peer_call(_peer_u_kernel, "peer_u", idx, (h2,), _pack_table(u_table[0]), (wide, wide))
    w_lo, w_hi = _peer_w(act_lo, act_hi, gate)
    (ffn,) = _peer_call(_peer_v_kernel, "peer_v", idx, (w_lo, w_hi), _pack_table(v_table[0]), (D_MODEL,))
    out = _final(h2, ffn, vec(ln_ffn_g[0]), vec(ln_ffn_b[0]))
    return out.reshape(batch, seq, d)
```

```python
import functools

import numpy as np
import jax
import jax.numpy as jnp
from jax import lax
from jax.experimental import pallas as pl
from jax.experimental.pallas import tpu as pltpu

F32 = jnp.float32
BF16 = jnp.bfloat16

D_MODEL = 1024
SSM_HEADS = 16
SSM_HEAD_DIM = 64
SSM_WIDTH = SSM_HEADS * SSM_HEAD_DIM
SSM_GROUPS = 2
SSM_STATE = 128
SSM_CONV = 4
SSM_CHUNK = 256
SSM_CONV_WIDTH = SSM_WIDTH + 2 * SSM_GROUPS * SSM_STATE
MLA_HEADS = 8
MLA_Q_RANK = 384
MLA_KV_RANK = 256
MLA_NOPE = 128
MLA_ROPE = 64
MLA_V = 128
MLA_WIDTH = MLA_HEADS * MLA_V
ROPE_THETA = 10000.0
PEER_HEADS = 8
PEER_KEYS = 128
PEER_HALF = 128
PEER_TOPK = 16
PEER_SEL = PEER_HEADS * PEER_TOPK
DEPTH = 1
ALPHA = (2.0 * DEPTH) ** 0.25
EPS = 1e-5

LANES = 128
SUBLANES = 8
MXU_DEPTH = 256
VMEM_LIMIT = 48 * 1024 * 1024
VMEM_LIMIT_TABLE = 56 * 1024 * 1024

ROW_TILE = 512
ATTN_BLOCK = 1024
ATTN_Q_TILE = 4 * ATTN_BLOCK
ROUTE_TILE = 512
GATHER_TILE = 512
GATHER_GROUP = SUBLANES
GROUPS_PER_PASS = 2
TILE_STRIDE = PEER_SEL + 1
WORDS_PER_ROW = D_MODEL // 2 // LANES

_OFF_Z = 0
_OFF_XBC = _OFF_Z + SSM_WIDTH
_OFF_QL = _OFF_XBC + SSM_CONV_WIDTH
_OFF_KVL = _OFF_QL + MLA_Q_RANK
_OFF_KPE = _OFF_KVL + MLA_KV_RANK
_OFF_DT = _OFF_KPE + LANES
_IN_COLS = _OFF_DT + LANES


def _params(n_grid):
    return pltpu.CompilerParams(dimension_semantics=("arbitrary",) * n_grid, vmem_limit_bytes=VMEM_LIMIT)


def _layer_norm(x, g, b):
    mu = jnp.mean(x, axis=-1, keepdims=True)
    xc = x - mu
    var = jnp.mean(xc * xc, axis=-1, keepdims=True)
    return xc * lax.rsqrt(var + EPS) * g + b


def _rms_norm(x, g):
    return x * lax.rsqrt(jnp.mean(x * x, axis=-1, keepdims=True) + EPS) * g


def _sigmoid(x):
    return 1.0 / (1.0 + jnp.exp(-x))


def _softplus(x):
    return jnp.maximum(x, 0.0) + jnp.log1p(jnp.exp(-jnp.abs(x)))


def _bf16_pieces(a):
    p0 = a.astype(BF16)
    r = a - p0.astype(F32)
    p1 = r.astype(BF16)
    return p0, p1, (r - p1.astype(F32)).astype(BF16)


def _dot_f32_by_01(a, sel, dims=(((1,), (0,)), ((), ()))):
    out = None
    for piece in _bf16_pieces(a):
        t = lax.dot_general(piece, sel, dims, preferred_element_type=F32)
        out = t if out is None else out + t
    return out


def _dot_01_by_f32(sel, a):
    out = None
    for piece in _bf16_pieces(a):
        t = jnp.dot(sel, piece, preferred_element_type=F32)
        out = t if out is None else out + t
    return out


def _inproj_kernel(x_ref, g_ref, b_ref, w_ref, z_ref, xbc_ref, ql_ref, kvl_ref, kpe_ref, dt_ref):
    h = _layer_norm(x_ref[...], g_ref[...], b_ref[...]).astype(BF16)

    def seg(lo, hi):
        return jnp.dot(h, w_ref[:, lo:hi], preferred_element_type=F32)

    z_ref[...] = seg(_OFF_Z, _OFF_XBC)
    xbc_ref[...] = seg(_OFF_XBC, _OFF_QL)
    ql_ref[...] = seg(_OFF_QL, _OFF_KVL)
    kvl_ref[...] = seg(_OFF_KVL, _OFF_KPE)
    kpe_ref[...] = seg(_OFF_KPE, _OFF_DT)
    dt_ref[...] = seg(_OFF_DT, _IN_COLS)


def _inproj(x2, ln_g, ln_b, w_packed):
    T = x2.shape[0]
    widths = (SSM_WIDTH, SSM_CONV_WIDTH, MLA_Q_RANK, MLA_KV_RANK, LANES, LANES)
    row = lambda w: pl.BlockSpec((ROW_TILE, w), lambda i: (i, 0))
    full = lambda a: pl.BlockSpec(a.shape, lambda i: (0,) * a.ndim)
    return pl.pallas_call(
        _inproj_kernel,
        grid=(T // ROW_TILE,),
        in_specs=[row(D_MODEL), full(ln_g), full(ln_b), full(w_packed)],
        out_specs=[row(w) for w in widths],
        out_shape=[jax.ShapeDtypeStruct((T, w), F32) for w in widths],
        compiler_params=_params(1),
        name="inproj",
    )(x2, ln_g, ln_b, w_packed)


def _ssd_kernel(z_ref, xbc_ref, dt_ref, dtT_ref, convw_ref, convb_ref, dtb_ref, dtbT_ref, alog_ref, alogT_ref,
                dskip_ref, g_ref, e_ref, y_ref, xpad_ref, st_ref):
    Q = SSM_CHUNK
    HALO = SUBLANES
    c = pl.program_id(1)

    @pl.when(c == 0)
    def _():
        xpad_ref[0:HALO, :] = jnp.zeros((HALO, SSM_CONV_WIDTH), F32)
        st_ref[...] = jnp.zeros(st_ref.shape, F32)

    raw = xbc_ref[...]
    xpad_ref[HALO:HALO + Q, :] = raw
    conv = convb_ref[...] + raw * convw_ref[SSM_CONV - 1:SSM_CONV, :]
    for k in range(SSM_CONV - 1):
        off = HALO - (SSM_CONV - 1) + k
        conv = conv + xpad_ref[off:off + Q, :] * convw_ref[k:k + 1, :]
    xpad_ref[0:HALO, :] = raw[Q - HALO:Q, :]
    xbc = conv * _sigmoid(conv)
    xs = xbc[:, :SSM_WIDTH]
    bm = xbc[:, SSM_WIDTH:SSM_WIDTH + SSM_GROUPS * SSM_STATE]
    cm = xbc[:, SSM_WIDTH + SSM_GROUPS * SSM_STATE:]

    dt = _softplus(dt_ref[...] + dtb_ref[...])
    dtT = _softplus(dtT_ref[...] + dtbT_ref[...])
    a = dt * (-jnp.exp(alog_ref[...]))
    aT = dtT * (-jnp.exp(alogT_ref[...]))
    row = lax.broadcasted_iota(jnp.int32, (Q, Q), 0)
    col = lax.broadcasted_iota(jnp.int32, (Q, Q), 1)
    causal = col <= row
    lower = jnp.where(causal, 1.0, 0.0).astype(BF16)
    upper = jnp.where(row <= col, 1.0, 0.0).astype(BF16)
    a_cum = _dot_01_by_f32(lower, a)
    a_cumT = _dot_f32_by_01(aT, upper)
    a_last = a_cum[Q - 1:Q, :]

    expand = e_ref[...]
    ex = lambda v: _dot_f32_by_01(v, expand)
    xdt = xs * ex(dt)
    decay_out = ex(jnp.exp(a_cum))
    decay_st = ex(jnp.exp(a_last - a_cum))
    chunk_decay = ex(jnp.broadcast_to(jnp.exp(a_last), (SUBLANES, LANES)))[0:1, :]
    xdt_b = xdt.astype(BF16)
    wx_b = (decay_st * xdt).astype(BF16)

    heads_per_group = SSM_HEADS // SSM_GROUPS
    gw = heads_per_group * SSM_HEAD_DIM
    y_parts = []
    for g in range(SSM_GROUPS):
        bg = bm[:, g * SSM_STATE:(g + 1) * SSM_STATE]
        cg = cm[:, g * SSM_STATE:(g + 1) * SSM_STATE].astype(BF16)
        cb = lax.dot_general(cg, bg.astype(BF16), (((1,), (1,)), ((), ())), preferred_element_type=F32)
        yd = []
        for e in range(heads_per_group):
            h = g * heads_per_group + e
            seg = a_cum[:, h:h + 1] - a_cumT[h:h + 1, :]
            m = (cb * jnp.exp(jnp.where(causal, seg, -jnp.inf))).astype(BF16)
            yd.append(jnp.dot(m, xdt_b[:, h * SSM_HEAD_DIM:(h + 1) * SSM_HEAD_DIM], preferred_element_type=F32))
        yd = jnp.concatenate(yd, axis=1)
        st = st_ref[g]
        y_off = jnp.dot(cg, st.astype(BF16), preferred_element_type=F32) * decay_out[:, g * gw:(g + 1) * gw]
        y_parts.append(yd + y_off)
        bgT = jnp.transpose(bg).astype(BF16)
        st_ref[g] = st * chunk_decay[:, g * gw:(g + 1) * gw] + jnp.dot(
            bgT, wx_b[:, g * gw:(g + 1) * gw], preferred_element_type=F32)
    y = jnp.concatenate(y_parts, axis=1)
    d_e = ex(jnp.broadcast_to(dskip_ref[...], (SUBLANES, LANES)))[0:1, :]
    y = y + d_e * xs
    zz = z_ref[...]
    y = y * (zz * _sigmoid(zz))
    y_ref[...] = _rms_norm(y, g_ref[...]).astype(BF16)


def _ssd(z, xbc, dt, dtT, conv_w, conv_b, dt_bias, a_log, d_skip, norm_g, batch, seq):
    T = z.shape[0]
    nc = seq // SSM_CHUNK
    pad = lambda v: jnp.pad(v.reshape(1, -1), ((0, 0), (0, LANES - v.shape[-1])))
    colv = lambda v: v.reshape(-1, 1)
    head_of_lane = np.arange(SSM_WIDTH) // SSM_HEAD_DIM
    expand = jnp.asarray((np.arange(LANES)[:, None] == head_of_lane[None, :]).astype(np.float32)).astype(BF16)
    row = lambda w: pl.BlockSpec((SSM_CHUNK, w), lambda b, c: (b * nc + c, 0))
    full = lambda a: pl.BlockSpec(a.shape, lambda b, c: (0,) * a.ndim)
    args = (z, xbc, dt, dtT, conv_w, conv_b.reshape(1, -1), pad(dt_bias), colv(dt_bias), pad(a_log), colv(a_log),
            pad(d_skip), norm_g.reshape(1, -1), expand)
    in_specs = [row(SSM_WIDTH), row(SSM_CONV_WIDTH), row(LANES),
                pl.BlockSpec((None, SSM_HEADS, SSM_CHUNK), lambda b, c: (b, 0, c))]
    in_specs += [full(a) for a in args[4:]]
    return pl.pallas_call(
        _ssd_kernel,
        grid=(batch, nc),
        in_specs=in_specs,
        out_specs=row(SSM_WIDTH),
        out_shape=jax.ShapeDtypeStruct((T, SSM_WIDTH), BF16),
        scratch_shapes=[pltpu.VMEM((SUBLANES + SSM_CHUNK, SSM_CONV_WIDTH), F32),
                        pltpu.VMEM((SSM_GROUPS, SSM_STATE, SSM_WIDTH // SSM_GROUPS), F32)],
        compiler_params=_params(2),
        name="ssd",
    )(*args)


def _mlaproj_kernel(ql_ref, kvl_ref, kpe_ref, cos_ref, sin_ref, qg_ref, kvg_ref, wqn_ref, wqp_ref, wqs_ref,
                    wkn_ref, wv_ref, q_ref, k_ref, v_ref):
    scale = (MLA_NOPE + MLA_ROPE) ** -0.5
    cos = cos_ref[...]
    sin = sin_ref[...]
    qn = _rms_norm(ql_ref[...], qg_ref[...]).astype(BF16)
    q_nope = jnp.dot(qn, wqn_ref[...], preferred_element_type=F32) * scale
    q_pe = (jnp.dot(qn, wqp_ref[...], preferred_element_type=F32) * cos
            + jnp.dot(qn, wqs_ref[...], preferred_element_type=F32) * sin) * scale
    kvn = _rms_norm(kvl_ref[...], kvg_ref[...]).astype(BF16)
    k_nope = jnp.dot(kvn, wkn_ref[...], preferred_element_type=F32)
    v = jnp.dot(kvn, wv_ref[...], preferred_element_type=F32)
    kpe = kpe_ref[...]
    k_pe = kpe[:, :MLA_ROPE] * cos[:, :MLA_ROPE] + kpe[:, MLA_ROPE:] * sin[:, :MLA_ROPE]
    zpad = jnp.zeros((q_nope.shape[0], MXU_DEPTH - MLA_NOPE - MLA_ROPE), F32)
    for h in range(MLA_HEADS):
        q_ref[h] = jnp.concatenate(
            [q_nope[:, h * MLA_NOPE:(h + 1) * MLA_NOPE], q_pe[:, h * MLA_ROPE:(h + 1) * MLA_ROPE], zpad],
            axis=1).astype(BF16)
        k_ref[h] = jnp.concatenate([k_nope[:, h * MLA_NOPE:(h + 1) * MLA_NOPE], k_pe, zpad], axis=1).astype(BF16)
        v_ref[h] = v[:, h * MLA_V:(h + 1) * MLA_V].astype(BF16)


def _mlaproj(ql, kvl, kpe, cosq, sinq, q_g, kv_g, wqn, wqp, wqs, wkn, wv, batch, seq):
    T = ql.shape[0]
    per_seq = seq // ROW_TILE
    row = lambda w: pl.BlockSpec((ROW_TILE, w), lambda i: (i, 0))
    pos = lambda w: pl.BlockSpec((ROW_TILE, w), lambda i: (i % per_seq, 0))
    full = lambda a: pl.BlockSpec(a.shape, lambda i: (0,) * a.ndim)
    head = lambda w: pl.BlockSpec((None, MLA_HEADS, ROW_TILE, w), lambda i: (i // per_seq, 0, i % per_seq, 0))
    consts = (q_g.reshape(1, -1), kv_g.reshape(1, -1), wqn, wqp, wqs, wkn, wv)
    return pl.pallas_call(
        _mlaproj_kernel,
        grid=(T // ROW_TILE,),
        in_specs=[row(MLA_Q_RANK), row(MLA_KV_RANK), row(LANES), pos(MLA_HEADS * MLA_ROPE), pos(MLA_HEADS * MLA_ROPE)]
        + [full(a) for a in consts],
        out_specs=[head(MXU_DEPTH), head(MXU_DEPTH), head(MLA_V)],
        out_shape=[jax.ShapeDtypeStruct((batch, MLA_HEADS, seq, MXU_DEPTH), BF16),
                   jax.ShapeDtypeStruct((batch, MLA_HEADS, seq, MXU_DEPTH), BF16),
                   jax.ShapeDtypeStruct((batch, MLA_HEADS, seq, MLA_V), BF16)],
        compiler_params=_params(1),
        name="mlaproj",
    )(ql, kvl, kpe, cosq, sinq, *consts)


def _flash_kernel(q_ref, k_ref, v_ref, o_ref, m_ref, l_ref, acc_ref):
    blk = ATTN_BLOCK
    halves = ATTN_Q_TILE // blk
    qi = pl.program_id(2)
    m_ref[...] = jnp.full(m_ref.shape, -jnp.inf, F32)
    l_ref[...] = jnp.zeros(l_ref.shape, F32)
    acc_ref[...] = jnp.zeros(acc_ref.shape, F32)
    below_diag = (lax.broadcasted_iota(jnp.int32, (blk, blk), 1) <= lax.broadcasted_iota(jnp.int32, (blk, blk), 0))

    def lane_tiles(x):
        return [x[:, j * LANES:(j + 1) * LANES] for j in range(x.shape[1] // LANES)]

    def step(half, kv_block, masked):
        rows = pl.ds(half * blk, blk)
        off = pl.multiple_of(kv_block * blk, blk)
        s = lax.dot_general(q_ref[rows, :], k_ref[pl.ds(off, blk), :], (((1,), (1,)), ((), ())),
                            preferred_element_type=F32)
        if masked:
            s = jnp.where(below_diag, s, -jnp.inf)
        m_prev = m_ref[rows, :]
        m_tile = functools.reduce(jnp.maximum, lane_tiles(s))
        m_new = jnp.maximum(m_prev, jnp.broadcast_to(jnp.max(m_tile, axis=-1, keepdims=True), (blk, LANES)))
        p = jnp.exp(s - jnp.tile(m_new, (1, blk // LANES)))
        alpha = jnp.exp(m_prev - m_new)
        p_tile = functools.reduce(lambda a, b: a + b, lane_tiles(p))
        l_ref[rows, :] = alpha * l_ref[rows, :] + jnp.broadcast_to(jnp.sum(p_tile, axis=-1, keepdims=True),
                                                                   (blk, LANES))
        acc_ref[rows, :] = alpha * acc_ref[rows, :] + jnp.dot(p.astype(BF16), v_ref[pl.ds(off, blk), :],
                                                              preferred_element_type=F32)
        m_ref[rows, :] = m_new

    def bulk(j, carry):
        for half in range(halves):
            step(half, j, False)
        return carry

    first = qi * halves
    lax.fori_loop(0, first, bulk, 0)
    for half in range(halves):
        for d in range(half):
            step(half, first + d, False)
        step(half, first + half, True)
    o_ref[...] = (acc_ref[...] / l_ref[...]).astype(BF16)


def _flash(q, k, v):
    batch, heads, seq, _ = q.shape
    qspec = pl.BlockSpec((None, None, ATTN_Q_TILE, MXU_DEPTH), lambda b, h, i: (b, h, i, 0))
    kspec = pl.BlockSpec((None, None, seq, MXU_DEPTH), lambda b, h, i: (b, h, 0, 0))
    vspec = pl.BlockSpec((None, None, seq, MLA_V), lambda b, h, i: (b, h, 0, 0))
    return pl.pallas_call(
        _flash_kernel,
        grid=(batch, heads, seq // ATTN_Q_TILE),
        in_specs=[qspec, kspec, vspec],
        out_specs=pl.BlockSpec((None, ATTN_Q_TILE, MLA_V), lambda b, h, i: (b, i, h)),
        out_shape=jax.ShapeDtypeStruct((batch, seq, heads * MLA_V), BF16),
        scratch_shapes=[pltpu.VMEM((ATTN_Q_TILE, LANES), F32), pltpu.VMEM((ATTN_Q_TILE, LANES), F32),
                        pltpu.VMEM((ATTN_Q_TILE, MLA_V), F32)],
        compiler_params=_params(3),
        name="flash",
    )(q, k, v)


def _outproj_kernel(x_ref, ys_ref, ym_ref, g0_ref, b0_ref, wa_ref, wb_ref, g1_ref, b1_ref, o_ref):
    h = _layer_norm(x_ref[...], g0_ref[...], b0_ref[...])
    mix = (jnp.dot(ys_ref[...], wa_ref[...], preferred_element_type=F32)
           + jnp.dot(ym_ref[...], wb_ref[...], preferred_element_type=F32))
    o_ref[...] = _layer_norm(ALPHA * h + mix, g1_ref[...], b1_ref[...])


def _outproj(x2, y_ssm, y_mla, g0, b0, wa, wb, g1, b1):
    T = x2.shape[0]
    row = lambda w: pl.BlockSpec((ROW_TILE, w), lambda i: (i, 0))
    full = lambda a: pl.BlockSpec(a.shape, lambda i: (0,) * a.ndim)
    consts = (g0, b0, wa, wb, g1, b1)
    return pl.pallas_call(
        _outproj_kernel,
        grid=(T // ROW_TILE,),
        in_specs=[row(D_MODEL), row(SSM_WIDTH), row(MLA_WIDTH)] + [full(a) for a in consts],
        out_specs=row(D_MODEL),
        out_shape=jax.ShapeDtypeStruct((T, D_MODEL), F32),
        compiler_params=_params(1),
        name="outproj",
    )(x2, y_ssm, y_mla, *consts)


def _top_rows(s, k, payload=None):
    n = s.shape[0]
    rows = lax.broadcasted_iota(jnp.int32, s.shape, 0).astype(F32)
    vals, ids = [], []
    for _ in range(k):
        m = jnp.max(s, axis=0, keepdims=True)
        first = jnp.min(jnp.where(s == m, rows, float(n)), axis=0, keepdims=True)
        hit = rows == first
        vals.append(m)
        ids.append(first if payload is None else jnp.sum(jnp.where(hit, payload, 0.0), axis=0, keepdims=True))
        s = jnp.where(hit, -jnp.inf, s)
    return vals, ids


def _pair_candidates(v1, v2, combine, pad):
    assert PEER_TOPK == 16
    v1s = jnp.concatenate(v1, axis=0)
    v2s = jnp.concatenate(v2, axis=0)
    pieces = [combine(v1[a], v2s[:nb]) for a, nb in enumerate((16, 8, 8, 4, 4, 2, 2, 2))]
    pieces.append(jnp.full((2, v2s.shape[1]), pad, F32))
    pieces.append(combine(v1s[SUBLANES:], v2[0]))
    return jnp.concatenate(pieces, axis=0)


def _route_kernel(h_ref, wq_ref, keys_ref, idx_ref, gate_ref, off_scr, gate_scr):
    hb = h_ref[...].astype(BF16)
    K = PEER_TOPK
    for h in range(PEER_HEADS):
        tops = []
        for half in range(2):
            c0 = (h * 2 + half) * PEER_HALF
            q = jnp.dot(hb, wq_ref[:, c0:c0 + PEER_HALF], preferred_element_type=F32)
            sT = lax.dot_general(keys_ref[h * 2 + half], q.astype(BF16), (((1,), (1,)), ((), ())),
                                 preferred_element_type=F32)
            tops.append(_top_rows(sT, K))
        (v1, i1), (v2, i2) = tops
        cand_s = _pair_candidates(v1, v2, lambda x, y: x + y, -jnp.inf)
        cand_e = _pair_candidates(i1, i2, lambda x, y: x * float(PEER_KEYS) + y, 0.0)
        top_s, top_e = _top_rows(cand_s, K, payload=cand_e)
        ex = [jnp.exp(s - top_s[0]) for s in top_s]
        denom = ex[0]
        for e in ex[1:]:
            denom = denom + e
        inv = 1.0 / denom
        for j in range(K):
            r = h * K + j
            off_scr[r:r + 1, :] = top_e[j] * float(WORDS_PER_ROW)
            gate_scr[r:r + 1, :] = ex[j] * inv
    idx_ref[...] = jnp.transpose(off_scr[...]).astype(jnp.int32)
    gate_ref[...] = jnp.transpose(gate_scr[...])


def _route(h2, wq, keys):
    T = h2.shape[0]
    full = lambda a: pl.BlockSpec(a.shape, lambda i: (0,) * a.ndim)
    out = pl.BlockSpec((ROUTE_TILE, PEER_SEL), lambda i: (i, 0))
    return pl.pallas_call(
        _route_kernel,
        grid=(T // ROUTE_TILE,),
        in_specs=[pl.BlockSpec((ROUTE_TILE, D_MODEL), lambda i: (i, 0)), full(wq), full(keys)],
        out_specs=[out, out],
        out_shape=[jax.ShapeDtypeStruct((T, PEER_SEL), jnp.int32), jax.ShapeDtypeStruct((T, PEER_SEL), F32)],
        scratch_shapes=[pltpu.VMEM((PEER_SEL, ROUTE_TILE), F32), pltpu.VMEM((PEER_SEL, ROUTE_TILE), F32)],
        compiler_params=_params(1),
        name="route",
    )(h2, wq, keys)


def _gather_group(idx_ref, tbl_ref, tile_ref, g):
    S = TILE_STRIDE
    for q in range(GATHER_GROUP):
        for kk in range(PEER_SEL):
            i = pl.multiple_of(idx_ref[g * GATHER_GROUP + q, kk], WORDS_PER_ROW)
            tile_ref[pl.ds(q * WORDS_PER_ROW * S + kk, WORDS_PER_ROW, stride=S), :] = tbl_ref[pl.ds(i, WORDS_PER_ROW), :]


def _tile_chunk(tile_ref, q, j):
    S = TILE_STRIDE
    return pltpu.bitcast(tile_ref[pl.ds(q * WORDS_PER_ROW * S + j * S, PEER_SEL), :], BF16)


def _pipelined_groups(idx_ref, tbl_ref, tile_a, tile_b, contract):
    groups = GATHER_TILE // GATHER_GROUP
    tiles = (tile_a, tile_b)
    _gather_group(idx_ref, tbl_ref, tile_a, 0)

    def several(p, carry):
        for k in range(GROUPS_PER_PASS):
            g = GROUPS_PER_PASS * p + k
            nxt = g + 1 if k + 1 < GROUPS_PER_PASS else jnp.minimum(g + 1, groups - 1)
            _gather_group(idx_ref, tbl_ref, tiles[(k + 1) % 2], nxt)
            contract(tiles[k % 2], g)
        return carry

    lax.fori_loop(0, groups // GROUPS_PER_PASS, several, 0)


def _keep_own_rows(acc, q, lo, hi):
    own = lax.broadcasted_iota(jnp.int32, lo.shape, 0) == q
    return jnp.where(own, acc[:GATHER_GROUP], lo), jnp.where(own, acc[GATHER_GROUP:], hi)


def _peer_u_kernel(idx_ref, x_ref, tbl_ref, lo_ref, hi_ref, tile_a, tile_b):
    half = D_MODEL // 2

    def contract(tile_ref, g):
        r0 = pl.multiple_of(GATHER_GROUP * g, GATHER_GROUP)
        x = x_ref[pl.ds(r0, GATHER_GROUP), :]
        lhs = jnp.concatenate([x[:, :half], x[:, half:]], axis=0).astype(BF16)
        lo = jnp.zeros((GATHER_GROUP, 2 * PEER_SEL), F32)
        hi = jnp.zeros((GATHER_GROUP, 2 * PEER_SEL), F32)
        for q in range(GATHER_GROUP):
            acc = jnp.zeros((2 * GATHER_GROUP, 2 * PEER_SEL), F32)
            for j in range(0, WORDS_PER_ROW, 2):
                rows2 = jnp.concatenate([_tile_chunk(tile_ref, q, j), _tile_chunk(tile_ref, q, j + 1)], axis=1)
                acc = acc + lax.dot_general(lhs[:, j * LANES:(j + 2) * LANES], rows2,
                                            (((1,), (1,)), ((), ())), preferred_element_type=F32)
            lo, hi = _keep_own_rows(acc, q, lo, hi)
        lo_ref[pl.ds(r0, GATHER_GROUP), :] = lo
        hi_ref[pl.ds(r0, GATHER_GROUP), :] = hi

    _pipelined_groups(idx_ref, tbl_ref, tile_a, tile_b, contract)


def _peer_v_kernel(idx_ref, wlo_ref, whi_ref, tbl_ref, o_ref, tile_a, tile_b):
    half = D_MODEL // 2

    def contract(tile_ref, g):
        r0 = pl.multiple_of(GATHER_GROUP * g, GATHER_GROUP)
        rows = pl.ds(r0, GATHER_GROUP)
        lhs = jnp.concatenate([wlo_ref[rows, :], whi_ref[rows, :]], axis=0).astype(BF16)
        lo = [jnp.zeros((GATHER_GROUP, LANES), F32)] * WORDS_PER_ROW
        hi = [jnp.zeros((GATHER_GROUP, LANES), F32)] * WORDS_PER_ROW
        for q in range(GATHER_GROUP):
            for j in range(WORDS_PER_ROW):
                r = jnp.dot(lhs, _tile_chunk(tile_ref, q, j), preferred_element_type=F32)
                lo[j], hi[j] = _keep_own_rows(r, q, lo[j], hi[j])
        o_ref[rows, :] = jnp.concatenate(lo + hi, axis=1)

    _pipelined_groups(idx_ref, tbl_ref, tile_a, tile_b, contract)


def _peer_call(kernel, name, idx, row_inputs, table, out_widths):
    T = idx.shape[0]
    tile = pltpu.VMEM((GATHER_GROUP * WORDS_PER_ROW * TILE_STRIDE, LANES), jnp.int32)
    row = lambda w: pl.BlockSpec((GATHER_TILE, w), lambda i: (i, 0))
    return pl.pallas_call(
        kernel,
        grid=(T // GATHER_TILE,),
        in_specs=[pl.BlockSpec((GATHER_TILE, PEER_SEL), lambda i: (i, 0), memory_space=pltpu.SMEM)]
        + [row(a.shape[1]) for a in row_inputs] + [pl.BlockSpec(memory_space=pltpu.VMEM)],
        out_specs=[row(w) for w in out_widths],
        out_shape=[jax.ShapeDtypeStruct((T, w), F32) for w in out_widths],
        scratch_shapes=[tile, tile],
        compiler_params=pltpu.CompilerParams(dimension_semantics=("arbitrary",), vmem_limit_bytes=VMEM_LIMIT_TABLE),
        name=name,
    )(idx, *row_inputs, table)


def _peer_w_kernel(lo_ref, hi_ref, gate_ref, sel_e_ref, sel_o_ref, wlo_ref, whi_ref):
    pick = (((1,), (1,)), ((), ()))
    act = _dot_f32_by_01(lo_ref[...], sel_e_ref[...], pick) + _dot_f32_by_01(hi_ref[...], sel_o_ref[...], pick)
    w = gate_ref[...] * (0.5 * act * (1.0 + lax.erf(act * (2.0 ** -0.5))))
    wlo_ref[...] = _dot_f32_by_01(w, sel_e_ref[...])
    whi_ref[...] = _dot_f32_by_01(w, sel_o_ref[...])


def _peer_w(act_lo, act_hi, gate):
    T = gate.shape[0]
    k = np.arange(PEER_SEL)
    sel_e = np.zeros((PEER_SEL, 2 * PEER_SEL), np.float32)
    sel_e[k, 2 * k] = 1.0
    sel_o = np.zeros((PEER_SEL, 2 * PEER_SEL), np.float32)
    sel_o[k, 2 * k + 1] = 1.0
    sel_e, sel_o = jnp.asarray(sel_e).astype(BF16), jnp.asarray(sel_o).astype(BF16)
    row = lambda w: pl.BlockSpec((ROW_TILE, w), lambda i: (i, 0))
    full = lambda a: pl.BlockSpec(a.shape, lambda i: (0,) * a.ndim)
    wide = 2 * PEER_SEL
    return pl.pallas_call(
        _peer_w_kernel,
        grid=(T // ROW_TILE,),
        in_specs=[row(wide), row(wide), row(PEER_SEL), full(sel_e), full(sel_o)],
        out_specs=[row(wide), row(wide)],
        out_shape=[jax.ShapeDtypeStruct((T, wide), F32), jax.ShapeDtypeStruct((T, wide), F32)],
        compiler_params=_params(1),
        name="peer_w",
    )(act_lo, act_hi, gate, sel_e, sel_o)


def _final_kernel(h_ref, f_ref, g_ref, b_ref, o_ref):
    o_ref[...] = _layer_norm(ALPHA * h_ref[...] + f_ref[...], g_ref[...], b_ref[...])


def _final(h2, ffn, g, b):
    T = h2.shape[0]
    row = pl.BlockSpec((ROW_TILE, D_MODEL), lambda i: (i, 0))
    full = lambda a: pl.BlockSpec(a.shape, lambda i: (0,) * a.ndim)
    return pl.pallas_call(
        _final_kernel,
        grid=(T // ROW_TILE,),
        in_specs=[row, row, full(g), full(b)],
        out_specs=row,
        out_shape=jax.ShapeDtypeStruct((T, D_MODEL), F32),
        compiler_params=_params(1),
        name="final",
    )(h2, ffn, g, b)


def _pack_table(tab):
    bits = lax.bitcast_convert_type(tab.astype(BF16), jnp.uint16).astype(jnp.uint32)
    half = D_MODEL // 2
    words = bits[:, :half] | (bits[:, half:] << 16)
    return lax.bitcast_convert_type(words, jnp.int32).reshape(-1, LANES)


def _rope_tables(seq):
    half = MLA_ROPE // 2
    inv_freq = ROPE_THETA ** (-jnp.arange(half, dtype=F32) / half)
    ang = jnp.arange(seq, dtype=F32)[:, None] * inv_freq
    cos, sin = jnp.cos(ang), jnp.sin(ang)
    cos64 = jnp.concatenate([cos, cos], axis=1)
    sin64 = jnp.concatenate([-sin, sin], axis=1)
    return jnp.tile(cos64, (1, MLA_HEADS)), jnp.tile(sin64, (1, MLA_HEADS))


def _swap_halves(w):
    half = w.shape[-1] // 2
    return jnp.concatenate([w[..., half:], w[..., :half]], axis=-1)


def kernel(x, ln_in_g, ln_in_b, w_in, conv_w, conv_b, dt_bias, a_log, d_skip, ssm_norm_g, q_norm_g, w_q_b, kv_norm_g, w_kv_b, w_out, ln_mix_g, ln_mix_b, w_query, sub_keys, u_table, v_table, ln_ffn_g, ln_ffn_b):
    batch, seq, d = x.shape
    assert d == D_MODEL and w_in.shape[0] == DEPTH == 1
    assert seq % ROW_TILE == 0 and seq % SSM_CHUNK == 0 and seq % ATTN_Q_TILE == 0
    T = batch * seq
    assert T % GATHER_TILE == 0 and T % ROUTE_TILE == 0
    x2 = x.reshape(T, d)
    vec = lambda v: v.reshape(1, -1)

    wi = w_in[0]
    o = np.cumsum((0, SSM_WIDTH, SSM_CONV_WIDTH, SSM_HEADS, MLA_Q_RANK, MLA_KV_RANK, MLA_ROPE))
    w_kpe = wi[:, o[5]:o[6]]
    w_packed = jnp.concatenate(
        [wi[:, o[0]:o[2]], wi[:, o[3]:o[5]], w_kpe, _swap_halves(w_kpe), wi[:, o[2]:o[3]],
         jnp.zeros((d, LANES - SSM_HEADS), F32)], axis=1).astype(BF16)
    z, xbc, ql, kvl, kpe, dt = _inproj(x2, vec(ln_in_g), vec(ln_in_b), w_packed)

    dtT = jnp.transpose(dt[:, :SSM_HEADS].reshape(batch, seq, SSM_HEADS), (0, 2, 1))
    y_ssm = _ssd(z, xbc, dt, dtT, conv_w[0], conv_b[0], dt_bias[0], a_log[0], d_skip[0], ssm_norm_g[0], batch, seq)

    wq = w_q_b[0].reshape(MLA_Q_RANK, MLA_HEADS, MLA_NOPE + MLA_ROPE)
    wq_pe = wq[:, :, MLA_NOPE:]
    wkv = w_kv_b[0].reshape(MLA_KV_RANK, MLA_HEADS, MLA_NOPE + MLA_V)
    flat = lambda w: w.reshape(w.shape[0], -1).astype(BF16)
    cosq, sinq = _rope_tables(seq)
    q, k, v = _mlaproj(ql, kvl, kpe, cosq, sinq, q_norm_g[0], kv_norm_g[0], flat(wq[:, :, :MLA_NOPE]), flat(wq_pe),
                       flat(_swap_halves(wq_pe)), flat(wkv[:, :, :MLA_NOPE]), flat(wkv[:, :, MLA_NOPE:]), batch, seq)
    y_mla = _flash(q, k, v).reshape(T, MLA_WIDTH)

    wo = w_out[0].astype(BF16)
    h2 = _outproj(x2, y_ssm, y_mla, vec(ln_in_g), vec(ln_in_b), wo[:SSM_WIDTH], wo[SSM_WIDTH:],
                  vec(ln_mix_g[0]), vec(ln_mix_b[0]))

    keys = sub_keys[0].reshape(PEER_HEADS * 2, PEER_KEYS, PEER_HALF).astype(BF16)
    idx, gate = _route(h2, w_query[0].astype(BF16), keys)

    wide = 2 * PEER_SEL
    act_lo, act_hi = _peer_call(_peer_u_kernel, "peer_u", idx, (h2,), _pack_table(u_table[0]), (wide, wide))
    w_lo, w_hi = _peer_w(act_lo, act_hi, gate)
    (ffn,) = _peer_call(_peer_v_kernel, "peer_v", idx, (w_lo, w_hi), _pack_table(v_table[0]), (D_MODEL,))
    out = _final(h2, ffn, vec(ln_ffn_g[0]), vec(ln_ffn_b[0]))
    return out.reshape(batch, seq, d)
```
